```python
import math
import jax, jax.numpy as jnp
from jax import lax
import numpy as np

D_MODEL = 1024
BATCH = 16
SEQ = 2048
DEPTH = 2

CTX_LEN = 256
GRID_W = 64
CHUNK = 64
NORM_EPS = 1e-6
ROPE_BASE = 10000.0

RET_HEADS = 4
RET_DK = 64
RET_DV = 128
HG_HEADS = 4
HG_DK = 128
HG_DV = 128
GDN_HEADS = 4
GDN_DK = 64
GDN_DV = 128
GDN_CONV = 4
LRU_WIDTH = 512
LRU_BLOCKS = 8
LRU_CONV = 4
LRU_C = 8.0

N_BRANCH = 4
BRANCH_WIDTH = 512
D_FF = 2816
FFN_CONV = 3
N_MOD = 6

IN_SPLITS = (
    ('ret_q', RET_HEADS * RET_DK),
    ('ret_k', RET_HEADS * RET_DK),
    ('ret_v', RET_HEADS * RET_DV),
    ('ret_g', RET_HEADS * RET_DV),
    ('hg_q', HG_HEADS * HG_DK),
    ('hg_f_fwd', HG_HEADS * HG_DK),
    ('hg_f_bwd', HG_HEADS * HG_DK),
    ('hg_i', HG_HEADS * HG_DV),
    ('hg_g', HG_HEADS * HG_DV),
    ('gdn_qkv', 2 * GDN_HEADS * GDN_DK + GDN_HEADS * GDN_DV),
    ('gdn_a', 2 * GDN_HEADS),
    ('gdn_b', 2 * GDN_HEADS),
    ('gdn_g', GDN_HEADS * GDN_DV),
    ('lru_x', LRU_WIDTH),
    ('lru_gate', LRU_WIDTH),
    ('merge_gate', N_BRANCH * D_MODEL),
)
N_IN = sum(w for _, w in IN_SPLITS)

kernel_name = 'hybrid_retention_hgrn2_gdn_rglru_block'


def _layernorm0(x):
    xf = x.astype(jnp.float32)
    mu = jnp.mean(xf, axis=-1, keepdims=True)
    var = jnp.mean(jnp.square(xf - mu), axis=-1, keepdims=True)
    return ((xf - mu) * lax.rsqrt(var + NORM_EPS)).astype(x.dtype)


def _layernorm(x, g, b):
    return _layernorm0(x) * g + b


def _modulate(x, shift, scale):
    return _layernorm0(x) * (1.0 + scale) + shift


def _gated_rmsnorm(o, w, gate):
    of = o.astype(jnp.float32)
    of = of * lax.rsqrt(jnp.mean(jnp.square(of), axis=-1, keepdims=True) + NORM_EPS)
    return of.astype(o.dtype) * w * jax.nn.silu(gate)


def _l2norm(z):
    zf = z.astype(jnp.float32)
    return zf * lax.rsqrt(jnp.sum(zf * zf, axis=-1, keepdims=True) + NORM_EPS)


def _depthwise_conv(x, w):
    k, ch = w.shape
    left = k // 2
    return lax.conv_general_dilated(
        x, w[:, None, :].astype(x.dtype), window_strides=(1,),
        padding=[(left, k - 1 - left)], dimension_numbers=('NWC', 'WIO', 'NWC'),
        feature_group_count=ch)


def _split_cols(z):
    out = {}
    off = 0
    for name, width in IN_SPLITS:
        out[name] = z[..., off:off + width]
        off += width
    return out


def _heads(z, n_heads):
    b, t, _ = z.shape
    return z.reshape(b, t, n_heads, -1).transpose(0, 2, 1, 3)


def _rope_2d(x, n_rows):
    dk = x.shape[-1]
    half, quarter = dk // 2, dk // 4
    inv_freq = ROPE_BASE ** (-jnp.arange(quarter, dtype=jnp.float32) / quarter)
    row = jnp.repeat(jnp.arange(n_rows, dtype=jnp.float32), GRID_W)
    col = (jnp.arange(n_rows * GRID_W) % GRID_W).astype(jnp.float32)

    def rotate(z, pos):
        ang = pos[:, None] * inv_freq[None, :]
        cos = jnp.cos(ang)[None, :, None, :].astype(z.dtype)
        sin = jnp.sin(ang)[None, :, None, :].astype(z.dtype)
        z1, z2 = z[..., :quarter], z[..., quarter:]
        return jnp.concatenate([z1 * cos - z2 * sin, z1 * sin + z2 * cos], axis=-1)

    return jnp.concatenate([rotate(x[..., :half], row), rotate(x[..., half:], col)], axis=-1)


def _chunk_gla(q, k, v, logf, s0):
    out_dtype = v.dtype
    q, k, v, logf = (a.astype(jnp.float32) for a in (q, k, v, logf))
    b, h, t, _ = q.shape
    n = t // CHUNK

    def chunks(z):
        return jnp.moveaxis(z.reshape(b, h, n, CHUNK, z.shape[-1]), 2, 0)

    lower = jnp.tril(jnp.ones((CHUNK, CHUNK), dtype=bool))

    def step(s, inp):
        qc, kc, vc, gc = inp
        cum = jnp.cumsum(gc, axis=2)
        rel = cum[:, :, :, None, :] - cum[:, :, None, :, :]
        decay = jnp.exp(jnp.where(lower[:, :, None], rel, -jnp.inf))
        scores = jnp.einsum('bhtd,bhsd,bhtsd->bhts', qc, kc, decay)
        o = (jnp.einsum('bhts,bhse->bhte', scores, vc)
             + jnp.einsum('bhtd,bhde->bhte', qc * jnp.exp(cum), s))
        last = cum[:, :, -1:, :]
        s_new = (jnp.exp(last[:, :, 0, :])[..., None] * s
                 + jnp.einsum('bhsd,bhse->bhde', kc * jnp.exp(last - cum), vc))
        return s_new, o

    s_fin, o = lax.scan(step, s0, (chunks(q), chunks(k), chunks(v), chunks(logf)))
    o = jnp.moveaxis(o, 0, 2).reshape(b, h, t, -1)
    return o.astype(out_dtype), s_fin


def _chunk_gated_delta(q, k, v, loga, beta, s0):
    out_dtype = v.dtype
    q, k, v, loga, beta = (a.astype(jnp.float32) for a in (q, k, v, loga, beta))
    b, h, t, _ = q.shape
    n = t // CHUNK

    def chunks(z):
        return jnp.moveaxis(z.reshape((b, h, n, CHUNK) + z.shape[3:]), 2, 0)

    incl = jnp.tril(jnp.ones((CHUNK, CHUNK), dtype=bool))
    strict = jnp.tril(jnp.ones((CHUNK, CHUNK), dtype=bool), k=-1)
    eye = jnp.eye(CHUNK, dtype=jnp.float32)

    def step(s, inp):
        qc, kc, vc, gc, bc = inp
        cum = jnp.cumsum(gc, axis=-1)
        rel = cum[..., :, None] - cum[..., None, :]
        a_mat = eye + (bc[..., :, None] * jnp.exp(jnp.where(strict, rel, -jnp.inf))
                       * jnp.einsum('bhtd,bhsd->bhts', kc, kc))
        rhs = bc[..., None] * (vc - jnp.exp(cum)[..., None] * jnp.einsum('bhtd,bhde->bhte', kc, s))
        u = lax.linalg.triangular_solve(a_mat, rhs, left_side=True, lower=True,
                                        unit_diagonal=True)
        qk = jnp.einsum('bhtd,bhsd->bhts', qc, kc) * jnp.exp(jnp.where(incl, rel, -jnp.inf))
        o = (jnp.exp(cum)[..., None] * jnp.einsum('bhtd,bhde->bhte', qc, s)
             + jnp.einsum('bhts,bhse->bhte', qk, u))
        last = cum[..., -1:]
        s_new = (jnp.exp(last)[..., None] * s
                 + jnp.einsum('bhsd,bhse->bhde', kc * jnp.exp(last - cum)[..., None], u))
        return s_new, o

    s_fin, o = lax.scan(step, s0, (chunks(q), chunks(k), chunks(v), chunks(loga), chunks(beta)))
    o = jnp.moveaxis(o, 0, 2).reshape(b, h, t, -1)
    return o.astype(out_dtype), s_fin


def _rglru_scan(log_a, xin, h0):
    out_dtype = xin.dtype
    log_a = log_a.astype(jnp.float32)
    xin = xin.astype(jnp.float32)
    a = jnp.exp(log_a)
    u = jnp.sqrt(-jnp.expm1(2.0 * log_a)) * xin

    def combine(e1, e2):
        a1, b1 = e1
        a2, b2 = e2
        return a1 * a2, a2 * b1 + b2

    a_cum, h_part = lax.associative_scan(combine, (a, u), axis=1)
    h = h_part + a_cum * h0[:, None, :]
    return h.astype(out_dtype), h[:, -1]


def _bidirectional(scan_fn, axis, s_zero, ctx_fwd, ctx_bwd, lat_fwd, lat_bwd):
    def flip(args):
        return tuple(jnp.flip(a, axis) for a in args)
    o_cf, s_cf = scan_fn(*ctx_fwd, s_zero)
    o_cb, s_cb = scan_fn(*flip(ctx_bwd), s_zero)
    o_lf, _ = scan_fn(*lat_fwd, s_cf)
    o_lb, _ = scan_fn(*flip(lat_bwd), s_cb)
    return o_cf + jnp.flip(o_cb, axis), o_lf + jnp.flip(o_lb, axis)


def _retention(zc, zl, n_rows):
    log_gamma = jnp.log1p(-jnp.exp2(-5.0 - jnp.arange(RET_HEADS, dtype=jnp.float32)))

    def prep(z, use_rope):
        b, t, _ = z['ret_q'].shape
        q = z['ret_q'].reshape(b, t, RET_HEADS, RET_DK)
        k = z['ret_k'].reshape(b, t, RET_HEADS, RET_DK) * (RET_DK ** -0.5)
        if use_rope:
            q = _rope_2d(q, n_rows)
            k = _rope_2d(k, n_rows)
        v = z['ret_v'].reshape(b, t, RET_HEADS, RET_DV)
        logf = jnp.broadcast_to(log_gamma[None, :, None, None], (b, RET_HEADS, t, RET_DK))
        return (q.transpose(0, 2, 1, 3), k.transpose(0, 2, 1, 3), v.transpose(0, 2, 1, 3), logf)

    args_c = prep(zc, False)
    args_l = prep(zl, True)
    s0 = jnp.zeros((zl['ret_q'].shape[0], RET_HEADS, RET_DK, RET_DV), jnp.float32)
    oc, ol = _bidirectional(_chunk_gla, 2, s0, args_c, args_c, args_l, args_l)

    def readout(o, z):
        o = _layernorm0(o.transpose(0, 2, 1, 3))
        gate = z['ret_g'].reshape(o.shape)
        return (o * jax.nn.silu(gate)).reshape(o.shape[0], o.shape[1], -1)

    return readout(oc, zc), readout(ol, zl)


def _hgrn2(zc, zl, lb_fwd, lb_bwd, norm_w):
    def prep(z):
        q = _heads(jax.nn.silu(z['hg_q']), HG_HEADS)
        v = _heads(z['hg_i'], HG_HEADS)

        def direction(f_logits, lb):
            f = lb + (1.0 - lb) * jax.nn.sigmoid(f_logits.astype(jnp.float32))
            return (q, _heads(1.0 - f, HG_HEADS), v, _heads(jnp.log(f), HG_HEADS))

        return direction(z['hg_f_fwd'], lb_fwd), direction(z['hg_f_bwd'], lb_bwd)

    (cf, cb), (lf, lbk) = prep(zc), prep(zl)
    s0 = jnp.zeros((zl['hg_q'].shape[0], HG_HEADS, HG_DK, HG_DV), jnp.float32)
    oc, ol = _bidirectional(_chunk_gla, 2, s0, cf, cb, lf, lbk)

    def readout(o, z):
        o = o.transpose(0, 2, 1, 3)
        gate = z['hg_g'].reshape(o.shape)
        return _gated_rmsnorm(o, norm_w, gate).reshape(o.shape[0], o.shape[1], -1)

    return readout(oc, zc), readout(ol, zl)


def _gdn(zc, zl, conv_w, a_log, dt_bias, norm_w):
    nk = GDN_HEADS * GDN_DK

    def prep(z):
        b, t, _ = z['gdn_qkv'].shape
        qkv = jax.nn.silu(_depthwise_conv(z['gdn_qkv'], conv_w))
        q = _l2norm(qkv[..., :nk].reshape(b, t, GDN_HEADS, GDN_DK)).transpose(0, 2, 1, 3) * (GDN_DK ** -0.5)
        k = _l2norm(qkv[..., nk:2 * nk].reshape(b, t, GDN_HEADS, GDN_DK)).transpose(0, 2, 1, 3)
        v = _heads(qkv[..., 2 * nk:], GDN_HEADS)
        a = z['gdn_a'].astype(jnp.float32).reshape(b, t, 2, GDN_HEADS)
        bl = z['gdn_b'].astype(jnp.float32).reshape(b, t, 2, GDN_HEADS)

        def direction(d):
            loga = -jnp.exp(a_log[d].astype(jnp.float32)) * jax.nn.softplus(a[:, :, d] + dt_bias[d])
            beta = jax.nn.sigmoid(bl[:, :, d])
            return (q, k, v, loga.transpose(0, 2, 1), beta.transpose(0, 2, 1))

        return direction(0), direction(1)

    (cf, cb), (lf, lbk) = prep(zc), prep(zl)
    s0 = jnp.zeros((zl['gdn_qkv'].shape[0], GDN_HEADS, GDN_DK, GDN_DV), jnp.float32)
    oc, ol = _bidirectional(_chunk_gated_delta, 2, s0, cf, cb, lf, lbk)

    def readout(o, z):
        o = o.transpose(0, 2, 1, 3)
        gate = z['gdn_g'].reshape(o.shape)
        return _gated_rmsnorm(o, norm_w, gate).reshape(o.shape[0], o.shape[1], -1)

    return readout(oc, zc), readout(ol, zl)


def _rglru(zc, zl, conv_w, conv_b, w_a, b_a, w_i, b_i, lam):
    def prep(z):
        b, t, _ = z['lru_x'].shape
        xc = _depthwise_conv(z['lru_x'], conv_w) + conv_b
        xb = xc.reshape(b, t, LRU_BLOCKS, LRU_WIDTH // LRU_BLOCKS)

        def direction(d):
            r = jax.nn.sigmoid(jnp.einsum('btnk,nkj->btnj', xb, w_a[d]).reshape(b, t, LRU_WIDTH) + b_a[d])
            i = jax.nn.sigmoid(jnp.einsum('btnk,nkj->btnj', xb, w_i[d]).reshape(b, t, LRU_WIDTH) + b_i[d])
            log_a = -LRU_C * r.astype(jnp.float32) * jax.nn.softplus(-lam[d].astype(jnp.float32))
            return (log_a, i * xc)

        return direction(0), direction(1)

    (cf, cb), (lf, lbk) = prep(zc), prep(zl)
    h0 = jnp.zeros((zl['lru_x'].shape[0], LRU_WIDTH), jnp.float32)
    hc, hl = _bidirectional(_rglru_scan, 1, h0, cf, cb, lf, lbk)
    return jax.nn.gelu(zc['lru_gate']) * hc, jax.nn.gelu(zl['lru_gate']) * hl


def _merge(branches, gate_logits, w_branch, w_out):
    b, t, _ = gate_logits.shape
    gates = jax.nn.sigmoid(gate_logits).reshape(b, t, N_BRANCH, D_MODEL)
    merged = gates[:, :, 0] * (branches[0] @ w_branch[0])
    for j in range(1, N_BRANCH):
        merged = merged + gates[:, :, j] * (branches[j] @ w_branch[j])
    return merged @ w_out


def _conv_ffn(h, w_up, conv_w, conv_b, w_down):
    u = _depthwise_conv(h @ w_up, conv_w) + conv_b
    val, gate = jnp.split(u, 2, axis=-1)
    return (jax.nn.silu(gate) * val) @ w_down


def setup_inputs(seed: int = 0) -> dict:
    key = jax.random.key(seed)
    ks = iter(jax.random.split(key, 40))
    f32 = jnp.float32
    beta_dn = (8.0 * DEPTH) ** -0.25
    bw = LRU_WIDTH // LRU_BLOCKS
    qkv_w = 2 * GDN_HEADS * GDN_DK + GDN_HEADS * GDN_DV

    def nrm(shape, std):
        return jax.random.normal(next(ks), shape, f32) * std

    def gain(shape):
        return 1.0 + nrm(shape, 0.02)

    x = nrm((BATCH, SEQ, D_MODEL), 1.0)
    c = nrm((BATCH, D_MODEL), 1.0)
    ctx = nrm((BATCH, CTX_LEN, D_MODEL), 1.0)
    c_ctx = nrm((D_MODEL,), 1.0)
    w_mod = nrm((DEPTH, D_MODEL, N_MOD * D_MODEL), 0.5 * D_MODEL ** -0.5)
    b_mod = nrm((DEPTH, N_MOD * D_MODEL), 0.01)
    w_in = nrm((DEPTH, D_MODEL, N_IN), D_MODEL ** -0.5)
    hg_lb = 1.0 + nrm((DEPTH, 2, HG_HEADS * HG_DK), 0.1)
    hg_norm = gain((DEPTH, HG_DV))
    gdn_conv = nrm((DEPTH, GDN_CONV, qkv_w), GDN_CONV ** -0.5)
    gdn_a_log = jnp.log(jax.random.uniform(next(ks), (DEPTH, 2, GDN_HEADS), f32, 1.0, 16.0))
    dt = jnp.exp(jax.random.uniform(next(ks), (DEPTH, 2, GDN_HEADS), f32, math.log(1e-3), math.log(1e-1)))
    gdn_dt_bias = dt + jnp.log(-jnp.expm1(-dt))
    gdn_norm = gain((DEPTH, GDN_DV))
    lru_conv = nrm((DEPTH, LRU_CONV, LRU_WIDTH), LRU_CONV ** -0.5)
    lru_conv_b = nrm((DEPTH, LRU_WIDTH), 0.01)
    lru_w_a = nrm((DEPTH, 2, LRU_BLOCKS, bw, bw), bw ** -0.5)
    lru_b_a = nrm((DEPTH, 2, LRU_WIDTH), 0.01)
    lru_w_i = nrm((DEPTH, 2, LRU_BLOCKS, bw, bw), bw ** -0.5)
    lru_b_i = nrm((DEPTH, 2, LRU_WIDTH), 0.01)
    a_c = jax.random.uniform(next(ks), (DEPTH, 2, LRU_WIDTH), f32, 0.9, 0.999)
    p = a_c ** (1.0 / LRU_C)
    lru_lam = jnp.log(p) - jnp.log1p(-p)
    w_branch = nrm((DEPTH, N_BRANCH, BRANCH_WIDTH, D_MODEL), BRANCH_WIDTH ** -0.5)
    w_out = nrm((DEPTH, D_MODEL, D_MODEL), beta_dn * D_MODEL ** -0.5)
    ln_mix_g = gain((DEPTH, D_MODEL))
    ln_mix_b = nrm((DEPTH, D_MODEL), 0.01)
    w_up = nrm((DEPTH, D_MODEL, 2 * D_FF), D_MODEL ** -0.5)
    ffn_conv = nrm((DEPTH, FFN_CONV, 2 * D_FF), FFN_CONV ** -0.5)
    ffn_conv_b = nrm((DEPTH, 2 * D_FF), 0.01)
    w_down = nrm((DEPTH, D_FF, D_MODEL), beta_dn * D_FF ** -0.5)
    ln_ffn_g = gain((DEPTH, D_MODEL))
    ln_ffn_b = nrm((DEPTH, D_MODEL), 0.01)
    return {
        'x': x, 'c': c, 'ctx': ctx, 'c_ctx': c_ctx,
        'w_mod': w_mod, 'b_mod': b_mod, 'w_in': w_in,
        'hg_lb': hg_lb, 'hg_norm': hg_norm,
        'gdn_conv': gdn_conv, 'gdn_a_log': gdn_a_log, 'gdn_dt_bias': gdn_dt_bias, 'gdn_norm': gdn_norm,
        'lru_conv': lru_conv, 'lru_conv_b': lru_conv_b, 'lru_w_a': lru_w_a, 'lru_b_a': lru_b_a,
        'lru_w_i': lru_w_i, 'lru_b_i': lru_b_i, 'lru_lam': lru_lam,
        'w_branch': w_branch, 'w_out': w_out, 'ln_mix_g': ln_mix_g, 'ln_mix_b': ln_mix_b,
        'w_up': w_up, 'ffn_conv': ffn_conv, 'ffn_conv_b': ffn_conv_b, 'w_down': w_down,
        'ln_ffn_g': ln_ffn_g, 'ln_ffn_b': ln_ffn_b,
    }


def reference(x, c, ctx, c_ctx, w_mod, b_mod, w_in, hg_lb, hg_norm, gdn_conv, gdn_a_log,
              gdn_dt_bias, gdn_norm, lru_conv, lru_conv_b, lru_w_a, lru_b_a, lru_w_i, lru_b_i,
              lru_lam, w_branch, w_out, ln_mix_g, ln_mix_b, w_up, ffn_conv, ffn_conv_b, w_down,
              ln_ffn_g, ln_ffn_b):
    alpha = (2.0 * DEPTH) ** 0.25
    n_rows = x.shape[1] // GRID_W
    lb = jnp.cumsum(jax.nn.softmax(hg_lb.astype(jnp.float32), axis=0), axis=0)
    lb = lb - lb[:1]
    h_ctx = ctx
    for i in range(DEPTH):
        mod_l = (jax.nn.silu(c) @ w_mod[i] + b_mod[i])[:, None, :]
        mod_c = (jax.nn.silu(c_ctx) @ w_mod[i] + b_mod[i])[None, None, :]
        sh1, sc1, g1, sh2, sc2, g2 = jnp.split(mod_l, N_MOD, axis=-1)
        csh1, csc1, cg1, csh2, csc2, cg2 = jnp.split(mod_c, N_MOD, axis=-1)

        zl = _split_cols(_modulate(x, sh1, sc1) @ w_in[i])
        zc = _split_cols(_modulate(h_ctx, csh1, csc1) @ w_in[i])
        ret_c, ret_l = _retention(zc, zl, n_rows)
        hg_c, hg_l = _hgrn2(zc, zl, lb[i, 0], lb[i, 1], hg_norm[i])
        gd_c, gd_l = _gdn(zc, zl, gdn_conv[i], gdn_a_log[i], gdn_dt_bias[i], gdn_norm[i])
        lr_c, lr_l = _rglru(zc, zl, lru_conv[i], lru_conv_b[i], lru_w_a[i], lru_b_a[i],
                            lru_w_i[i], lru_b_i[i], lru_lam[i])
        y_l = _merge([ret_l, hg_l, gd_l, lr_l], zl['merge_gate'], w_branch[i], w_out[i])
        x = _layernorm(alpha * x + g1 * y_l, ln_mix_g[i], ln_mix_b[i])

        f_l = _conv_ffn(_modulate(x, sh2, sc2), w_up[i], ffn_conv[i], ffn_conv_b[i], w_down[i])
        x = _layernorm(alpha * x + g2 * f_l, ln_ffn_g[i], ln_ffn_b[i])

        if i < DEPTH - 1:
            y_c = _merge([ret_c, hg_c, gd_c, lr_c], zc['merge_gate'], w_branch[i], w_out[i])
            h_ctx = _layernorm(alpha * h_ctx + cg1 * y_c, ln_mix_g[i], ln_mix_b[i])
            f_c = _conv_ffn(_modulate(h_ctx, csh2, csc2), w_up[i], ffn_conv[i], ffn_conv_b[i], w_down[i])
            h_ctx = _layernorm(alpha * h_ctx + cg2 * f_c, ln_ffn_g[i], ln_ffn_b[i])
    return x
```

```python
import functools
import math

import numpy as np
import jax
import jax.numpy as jnp
from jax import lax
from jax.experimental import pallas as pl
from jax.experimental.pallas import tpu as pltpu

F32 = jnp.float32
BF16 = jnp.bfloat16

D_MODEL = 1024
GRID_W = 64
NORM_EPS = 1e-6
ROPE_BASE = 10000.0
RET_HEADS, RET_DK, RET_DV = 4, 64, 128
HG_HEADS, HG_DK, HG_DV = 4, 128, 128
GDN_HEADS, GDN_DK, GDN_DV, GDN_CONV = 4, 64, 128, 4
LRU_WIDTH, LRU_BLOCKS, LRU_CONV, LRU_C = 512, 8, 4, 8.0
N_BRANCH, BRANCH_WIDTH = 4, 512
D_FF, FFN_CONV = 2816, 3
N_MOD = 6

LANES = 128
HALO = 16
VMEM_LIMIT = 56 * 1024 * 1024

BLK_MG = 0
BLK_LRU_X = 32
BLK_LRU_G = 36
BLK_HG_Q = 40
BLK_HG_FF = 44
BLK_HG_FB = 48
BLK_HG_I = 52
BLK_HG_G = 56
BLK_RET_QK = 60
BLK_RET_V = 64
BLK_RET_G = 68
BLK_GDN_QK = 72
BLK_GDN_V = 76
BLK_GDN_G = 80
BLK_GDN_MISC = 84
N_BLK = 85
N_PROJ = N_BLK * LANES

RET_CHUNK = 128
GLA_CHUNK = 64
LRU_ROWS = 256


def _dot(a, b):
    return jnp.dot(a, b, preferred_element_type=F32)


def _dot_nt(a, b):
    return lax.dot_general(a, b, (((1,), (1,)), ((), ())), preferred_element_type=F32)


def _dot_tn(a, b):
    return lax.dot_general(a, b, (((0,), (0,)), ((), ())), preferred_element_type=F32)


def _silu(x):
    return x * jax.nn.sigmoid(x)


def _split3(x):
    hi = x.astype(BF16)
    r1 = x - hi.astype(F32)
    mid = r1.astype(BF16)
    lo = (r1 - mid.astype(F32)).astype(BF16)
    return hi, mid, lo


def _ln0(x):
    mu = jnp.mean(x, axis=-1, keepdims=True)
    xc = x - mu
    var = jnp.mean(xc * xc, axis=-1, keepdims=True)
    return xc * lax.rsqrt(var + NORM_EPS)


def _row_ids(shape, start):
    return start + lax.broadcasted_iota(jnp.int32, shape, 0)


def _chunk_order(it, n_ctx, n_all):
    cb = jnp.where(it < n_ctx, n_ctx - 1 - it, n_all + n_ctx - 1 - it)
    return it, cb


def _level_structs(c):
    t = np.arange(c)
    j = t[None, :]
    segs, masks = [], []
    half = c // 2
    while half >= 1:
        blk = t // (2 * half)
        ref = blk * 2 * half + half - 1
        upper = (t - blk * 2 * half) >= half
        seg_u = (j > ref[:, None]) & (j <= t[:, None])
        seg_l = (j > t[:, None]) & (j <= ref[:, None])
        segs.append(np.where(upper[:, None], seg_u, seg_l))
        masks.append((blk[:, None] == blk[None, :]) & upper[:, None] & (~upper)[None, :])
        half //= 2
    return segs, masks


def _cum_rows(c):
    t = np.arange(c)
    j = t[None, :]
    incl = j <= t[:, None]
    rest = j > t[:, None]
    ones = np.ones((8, c), bool)
    return [incl, rest, ones]


def _flip2(m):
    return m[::-1, ::-1]


@functools.lru_cache(maxsize=None)
def _hg_consts(c):
    segs, masks = _level_structs(c)
    rows_f = segs + _cum_rows(c)
    rows_b = [_flip2(m) for m in segs] + [_flip2(m) for m in _cum_rows(c)[:2]] + _cum_rows(c)[2:]
    masks = masks + [np.eye(c, dtype=bool)]
    a_f = np.concatenate(rows_f, axis=0).astype(np.float32)
    a_b = np.concatenate(rows_b, axis=0).astype(np.float32)
    a3 = np.stack([np.tile(a_f, (1, 3)), np.tile(a_b, (1, 3))])
    m_f = np.stack(masks).astype(np.float32)
    m_b = np.stack([_flip2(m) for m in masks]).astype(np.float32)
    return a3, np.stack([m_f, m_b])


@functools.lru_cache(maxsize=None)
def _gdn_consts(c):
    _, masks = _level_structs(c)
    masks = masks[::-1]
    t = np.arange(c)
    incl = t[None, :] <= t[:, None]
    strict = t[None, :] < t[:, None]
    a_f = np.concatenate(_cum_rows(c), axis=0).astype(np.float32)
    a_b = np.concatenate([_flip2(m) for m in _cum_rows(c)[:2]] + _cum_rows(c)[2:], axis=0).astype(np.float32)
    a3 = np.stack([np.tile(a_f, (1, 3)), np.tile(a_b, (1, 3))])
    lv_f = np.stack(masks).astype(np.float32)
    lv_b = np.stack([_flip2(m) for m in masks]).astype(np.float32)
    tri = np.stack([np.stack([incl, strict]), np.stack([_flip2(incl), _flip2(strict)])]).astype(np.float32)
    return a3, np.stack([lv_f, lv_b]), tri


def _proj_perm():
    names = (('ret_q', 256), ('ret_k', 256), ('ret_v', 512), ('ret_g', 512),
             ('hg_q', 512), ('hg_f_fwd', 512), ('hg_f_bwd', 512), ('hg_i', 512), ('hg_g', 512),
             ('gdn_qkv', 1024), ('gdn_a', 8), ('gdn_b', 8), ('gdn_g', 512),
             ('lru_x', 512), ('lru_gate', 512), ('merge_gate', 4096))
    off, o = {}, 0
    for name, w in names:
        off[name] = o
        o += w
    n_in = o
    perm = np.full((N_PROJ,), n_in, np.int64)

    def put(blk, src, width):
        perm[blk * LANES: blk * LANES + width] = np.arange(src, src + width)

    put(BLK_MG, off['merge_gate'], 4096)
    put(BLK_LRU_X, off['lru_x'], 512)
    put(BLK_LRU_G, off['lru_gate'], 512)
    put(BLK_HG_Q, off['hg_q'], 512)
    put(BLK_HG_FF, off['hg_f_fwd'], 512)
    put(BLK_HG_FB, off['hg_f_bwd'], 512)
    put(BLK_HG_I, off['hg_i'], 512)
    put(BLK_HG_G, off['hg_g'], 512)
    for h in range(RET_HEADS):
        perm[(BLK_RET_QK + h) * LANES: (BLK_RET_QK + h) * LANES + 64] = off['ret_q'] + 64 * h + np.arange(64)
        perm[(BLK_RET_QK + h) * LANES + 64: (BLK_RET_QK + h + 1) * LANES] = off['ret_k'] + 64 * h + np.arange(64)
    put(BLK_RET_V, off['ret_v'], 512)
    put(BLK_RET_G, off['ret_g'], 512)
    gq = off['gdn_qkv']
    for h in range(GDN_HEADS):
        perm[(BLK_GDN_QK + h) * LANES: (BLK_GDN_QK + h) * LANES + 64] = gq + 64 * h + np.arange(64)
        perm[(BLK_GDN_QK + h) * LANES + 64: (BLK_GDN_QK + h + 1) * LANES] = gq + 256 + 64 * h + np.arange(64)
    put(BLK_GDN_V, gq + 512, 512)
    put(BLK_GDN_G, off['gdn_g'], 512)
    put(BLK_GDN_MISC, off['gdn_a'], 8)
    perm[BLK_GDN_MISC * LANES + 8: BLK_GDN_MISC * LANES + 16] = off['gdn_b'] + np.arange(8)
    return perm, n_in


def _gdn_conv_perm():
    p = np.zeros((1024,), np.int64)
    for h in range(GDN_HEADS):
        p[128 * h: 128 * h + 64] = 64 * h + np.arange(64)
        p[128 * h + 64: 128 * h + 128] = 256 + 64 * h + np.arange(64)
    p[512:] = 512 + np.arange(512)
    return p


def _mod_kernel(c_ref, w_ref, b_ref, o_ref):
    s = _silu(c_ref[...])
    o_ref[0] = jnp.dot(s, w_ref[0], preferred_element_type=F32,
                       precision=lax.Precision.HIGHEST) + b_ref[0]


def _modulation(cc, w_mod, b_mod):
    depth, d, n = w_mod.shape
    rows = cc.shape[0]
    tn = 1024
    return pl.pallas_call(
        _mod_kernel,
        grid=(depth, n // tn),
        in_specs=[pl.BlockSpec((rows, d), lambda l, j: (0, 0)),
                  pl.BlockSpec((1, d, tn), lambda l, j: (l, 0, j)),
                  pl.BlockSpec((1, 1, tn), lambda l, j: (l, 0, j))],
        out_specs=pl.BlockSpec((1, rows, tn), lambda l, j: (l, 0, j)),
        out_shape=jax.ShapeDtypeStruct((depth, rows, n), F32),
        compiler_params=pltpu.CompilerParams(vmem_limit_bytes=VMEM_LIMIT),
        name="modulation",
    )(cc, w_mod, b_mod.reshape(depth, 1, n))


def _modulated(x, rows, ctx_len, sh_l, sc_l, sh_c, sc_c):
    is_ctx = rows < ctx_len
    scale = jnp.where(is_ctx, sc_c, sc_l)
    shift = jnp.where(is_ctx, sh_c, sh_l)
    return _ln0(x) * (1.0 + scale) + shift


def _inproj_kernel(x_ref, shl_ref, scl_ref, shc_ref, scc_ref, w_ref, o_ref, xm_ref, *, ctx_len, tm):
    t = pl.program_id(1)

    @pl.when(pl.program_id(2) == 0)
    def _():
        rows = _row_ids((tm, 1), t * tm)
        xm = _modulated(x_ref[0], rows, ctx_len, shl_ref[0], scl_ref[0], shc_ref[...], scc_ref[...])
        xm_ref[...] = xm.astype(BF16)

    o_ref[0] = _dot(xm_ref[...], w_ref[...]).astype(o_ref.dtype)


def _pick_tile(n, cands):
    for c in cands:
        if n % c == 0:
            return c
    return n


def _in_projection(h, sh_l, sc_l, sh_c, sc_c, w_p, ctx_len):
    b, s, d = h.shape
    n = w_p.shape[1]
    tm = _pick_tile(s, (1152, 768, 512, 384, 256, 128))
    tn = _pick_tile(n, (2176, 1280, 640, 128))
    kern = functools.partial(_inproj_kernel, ctx_len=ctx_len, tm=tm)
    return pl.pallas_call(
        kern,
        grid=(b, s // tm, n // tn),
        in_specs=[pl.BlockSpec((1, tm, d), lambda i, t, j: (i, t, 0)),
                  pl.BlockSpec((1, 1, d), lambda i, t, j: (i, 0, 0)),
                  pl.BlockSpec((1, 1, d), lambda i, t, j: (i, 0, 0)),
                  pl.BlockSpec((1, d), lambda i, t, j: (0, 0)),
                  pl.BlockSpec((1, d), lambda i, t, j: (0, 0)),
                  pl.BlockSpec((d, tn), lambda i, t, j: (0, j))],
        out_specs=pl.BlockSpec((1, tm, tn), lambda i, t, j: (i, t, j)),
        out_shape=jax.ShapeDtypeStruct((b, s, n), BF16),
        scratch_shapes=[pltpu.VMEM((tm, d), BF16)],
        compiler_params=pltpu.CompilerParams(
            dimension_semantics=("parallel", "parallel", "arbitrary"),
            vmem_limit_bytes=VMEM_LIMIT),
        name="in_projection",
    )(h, sh_l, sc_l, sh_c, sc_c, w_p)


def _ret_kernel(lg_ref, qk_ref, v_ref, g_ref, cos_ref, sin_ref, o_ref, qkr_ref, of_ref, ob_ref,
                *, ctx_len, seq):
    c = RET_CHUNK
    n_all, n_ctx = seq // c, ctx_len // c
    lane = lax.broadcasted_iota(jnp.int32, (c, LANES), 1)
    lg = lg_ref[0]

    def rope(ci, carry):
        rows = pl.ds(pl.multiple_of(ci * c, c), c)
        x = qk_ref[0, rows, :].astype(F32)
        swapped = jnp.where(lane % 32 < 16, pltpu.roll(x, LANES - 16, 1), pltpu.roll(x, 16, 1))
        qkr_ref[rows, :] = x * cos_ref[rows, :] + swapped * sin_ref[rows, :]
        return carry

    lax.fori_loop(0, n_all, rope, 0)

    pos = lax.broadcasted_iota(jnp.int32, (c, LANES), 0).astype(F32)
    is_q = lane < RET_DK
    fac_f = jnp.exp(jnp.where(is_q, pos + 1.0, c - 1.0 - pos) * lg)
    fac_b = jnp.exp(jnp.where(is_q, c - pos, pos) * lg)
    rr = lax.broadcasted_iota(jnp.int32, (c, c), 0)
    cc = lax.broadcasted_iota(jnp.int32, (c, c), 1)
    dist = jnp.abs(rr - cc).astype(F32)
    dmat = jnp.exp(dist * lg[:, :c]) * jnp.where(rr == cc, 2.0, 1.0)
    gc = jnp.exp(lg * float(c))

    def step(it, carry):
        s_f, s_b = carry
        cf, cb = _chunk_order(it, n_ctx, n_all)
        rows_f = pl.ds(pl.multiple_of(cf * c, c), c)
        rows_b = pl.ds(pl.multiple_of(cb * c, c), c)

        qk = qkr_ref[rows_f, :]
        v = v_ref[0, rows_f, :]
        q = qk[:, :RET_DK].astype(BF16)
        k = qk[:, RET_DK:].astype(BF16)
        p = (_dot_nt(q, k) * dmat).astype(BF16)
        qkt = qk * fac_f
        o = _dot(p, v) + _dot(qkt[:, :RET_DK].astype(BF16), s_f.astype(BF16))
        s_f = gc * s_f + _dot_tn(qkt[:, RET_DK:].astype(BF16), v)
        of_ref[rows_f, :] = o

        qkb = qkr_ref[rows_b, :] * fac_b
        vb = v_ref[0, rows_b, :]
        ob_ref[rows_b, :] = _dot(qkb[:, :RET_DK].astype(BF16), s_b.astype(BF16))
        s_b = gc * s_b + _dot_tn(qkb[:, RET_DK:].astype(BF16), vb)
        return s_f, s_b

    zero = jnp.zeros((RET_DK, RET_DV), F32)
    lax.fori_loop(0, n_all, step, (zero, zero))

    def readout(ci, carry):
        rows = pl.ds(pl.multiple_of(ci * c, c), c)
        o = _ln0(of_ref[rows, :] + ob_ref[rows, :])
        o_ref[0, rows, :] = (o * _silu(g_ref[0, rows, :].astype(F32))).astype(o_ref.dtype)
        return carry

    lax.fori_loop(0, n_all, readout, 0)


def _retention(z, lg_rows, cos_t, sin_t, ctx_len):
    b, s, _ = z.shape
    kern = functools.partial(_ret_kernel, ctx_len=ctx_len, seq=s)

    def col(blk):
        return pl.BlockSpec((1, s, LANES), lambda i, h: (i, 0, blk + h))

    return pl.pallas_call(
        kern,
        grid=(b, RET_HEADS),
        in_specs=[pl.BlockSpec((1, 1, LANES), lambda i, h: (h, 0, 0)),
                  col(BLK_RET_QK), col(BLK_RET_V), col(BLK_RET_G),
                  pl.BlockSpec((s, LANES), lambda i, h: (0, 0)),
                  pl.BlockSpec((s, LANES), lambda i, h: (0, 0))],
        out_specs=pl.BlockSpec((1, s, LANES), lambda i, h: (i, 0, h)),
        out_shape=jax.ShapeDtypeStruct((b, s, RET_HEADS * RET_DV), BF16),
        scratch_shapes=[pltpu.VMEM((s, LANES), F32), pltpu.VMEM((s, LANES), F32),
                        pltpu.VMEM((s, LANES), F32)],
        compiler_params=pltpu.CompilerParams(
            dimension_semantics=("parallel", "parallel"), vmem_limit_bytes=VMEM_LIMIT),
        name="retention",
    )(lg_rows, z, z, z, cos_t, sin_t)


def _hg_local(q, logits, lb, v, a3_ref, m_ref, d):
    c = GLA_CHUNK
    nl = m_ref.shape[1] - 1
    f = lb + (1.0 - lb) * jax.nn.sigmoid(logits)
    g = jnp.log(f)
    kk = 1.0 - f
    g3 = jnp.concatenate(_split3(g), axis=0)
    r = _dot(a3_ref[d], g3)
    scores = m_ref[d, nl] * _dot_nt(q.astype(BF16), kk.astype(BF16))
    for l in range(nl):
        e = jnp.exp(r[l * c:(l + 1) * c])
        scores = scores + m_ref[d, l] * _dot_nt((q * e).astype(BF16), (kk * e).astype(BF16))
    o = _dot(scores.astype(BF16), v)
    cum = r[nl * c:(nl + 1) * c]
    rest = r[(nl + 1) * c:(nl + 2) * c]
    total = r[(nl + 2) * c:(nl + 2) * c + 8]
    return o, q * jnp.exp(cum), kk * jnp.exp(rest), jnp.exp(total)


def _hg_kernel(q_ref, ff_ref, fb_ref, i_ref, g_ref, lbf_ref, lbb_ref, nw_ref, a3_ref, m_ref, o_ref,
               acc_ref, qf_ref, qb_ref, kf_ref, kb_ref, df_ref, db_ref, *, ctx_len, seq):
    c = GLA_CHUNK
    n_all, n_ctx = seq // c, ctx_len // c

    def local(ci, carry):
        rows = pl.ds(pl.multiple_of(ci * c, c), c)
        q = _silu(q_ref[0, rows, :].astype(F32))
        v = i_ref[0, rows, :]
        o_f, qt_f, kt_f, d_f = _hg_local(q, ff_ref[0, rows, :].astype(F32), lbf_ref[...], v,
                                         a3_ref, m_ref, 0)
        o_b, qt_b, kt_b, d_b = _hg_local(q, fb_ref[0, rows, :].astype(F32), lbb_ref[...], v,
                                         a3_ref, m_ref, 1)
        acc_ref[rows, :] = o_f + o_b
        qf_ref[rows, :] = qt_f.astype(BF16)
        qb_ref[rows, :] = qt_b.astype(BF16)
        kf_ref[rows, :] = kt_f.astype(BF16)
        kb_ref[rows, :] = kt_b.astype(BF16)
        df_ref[ci] = d_f
        db_ref[ci] = d_b
        return carry

    lax.fori_loop(0, n_all, local, 0)

    def step(it, carry):
        st_f, st_b = carry
        cf, cb = _chunk_order(it, n_ctx, n_all)
        rows_f = pl.ds(pl.multiple_of(cf * c, c), c)
        rows_b = pl.ds(pl.multiple_of(cb * c, c), c)
        acc_ref[rows_f, :] += _dot_nt(qf_ref[rows_f, :], st_f.astype(BF16))
        st_f = st_f * df_ref[cf][0:1] + _dot_tn(i_ref[0, rows_f, :], kf_ref[rows_f, :])
        acc_ref[rows_b, :] += _dot_nt(qb_ref[rows_b, :], st_b.astype(BF16))
        st_b = st_b * db_ref[cb][0:1] + _dot_tn(i_ref[0, rows_b, :], kb_ref[rows_b, :])
        return st_f, st_b

    zero = jnp.zeros((HG_DV, HG_DK), F32)
    lax.fori_loop(0, n_all, step, (zero, zero))

    def readout(ci, carry):
        rows = pl.ds(pl.multiple_of(ci * c, c), c)
        o = acc_ref[rows, :]
        o = o * lax.rsqrt(jnp.mean(o * o, axis=-1, keepdims=True) + NORM_EPS)
        o_ref[0, rows, :] = (o * nw_ref[...] * _silu(g_ref[0, rows, :].astype(F32))).astype(o_ref.dtype)
        return carry

    lax.fori_loop(0, n_all, readout, 0)


def _hgrn2(z, lb_f, lb_b, norm_w, ctx_len):
    b, s, _ = z.shape
    c = GLA_CHUNK
    a3, masks = _hg_consts(c)
    a3 = jnp.asarray(a3, BF16)
    masks = jnp.asarray(masks, F32)
    kern = functools.partial(_hg_kernel, ctx_len=ctx_len, seq=s)

    def col(blk):
        return pl.BlockSpec((1, s, LANES), lambda i, h: (i, 0, blk + h))

    def full(a):
        nd = a.ndim
        return pl.BlockSpec(a.shape, lambda i, h: (0,) * nd)

    return pl.pallas_call(
        kern,
        grid=(b, HG_HEADS),
        in_specs=[col(BLK_HG_Q), col(BLK_HG_FF), col(BLK_HG_FB), col(BLK_HG_I), col(BLK_HG_G),
                  pl.BlockSpec((1, LANES), lambda i, h: (0, h)),
                  pl.BlockSpec((1, LANES), lambda i, h: (0, h)),
                  pl.BlockSpec((1, LANES), lambda i, h: (0, 0)),
                  full(a3), full(masks)],
        out_specs=pl.BlockSpec((1, s, LANES), lambda i, h: (i, 0, h)),
        out_shape=jax.ShapeDtypeStruct((b, s, HG_HEADS * HG_DV), BF16),
        scratch_shapes=[pltpu.VMEM((s, LANES), F32)] + [pltpu.VMEM((s, LANES), BF16)] * 4
                       + [pltpu.VMEM((s // c, 8, LANES), F32)] * 2,
        compiler_params=pltpu.CompilerParams(
            dimension_semantics=("parallel", "parallel"), vmem_limit_bytes=VMEM_LIMIT),
        name="hgrn2",
    )(z, z, z, z, z, lb_f, lb_b, norm_w, a3, masks)


def _conv_rows(ref, r0, c, seq, ctx_len, w, taps_left):
    lo = pl.multiple_of(jnp.maximum(r0 - HALO, 0), HALO)
    hi = pl.multiple_of(jnp.minimum(r0 + c, seq - HALO), HALO)
    xh = jnp.concatenate([ref[0, pl.ds(lo, HALO), :].astype(F32),
                          ref[0, pl.ds(pl.multiple_of(r0, HALO), c), :].astype(F32),
                          ref[0, pl.ds(hi, HALO), :].astype(F32)], axis=0)
    rows = _row_ids((c + 2 * HALO, 1), r0 - HALO)
    seg_lo = jnp.where(r0 < ctx_len, 0, ctx_len)
    seg_hi = jnp.where(r0 < ctx_len, ctx_len, seq)
    xh = jnp.where((rows >= seg_lo) & (rows < seg_hi), xh, 0.0)
    out = None
    for j in range(w.shape[0]):
        start = HALO + j - taps_left
        term = xh[start:start + c] * w[j:j + 1]
        out = term if out is None else out + term
    return out


def _gdn_kernel(qk_ref, v_ref, g_ref, misc_ref, cw_qk_ref, cw_v_ref, par_ref, nw_ref,
                a3_ref, lv_ref, tri_ref, o_ref,
                qkn_ref, va_ref, acc_ref, w_ref, u0_ref, qkd_ref, qt_ref, ktt_ref, el_ref,
                *, ctx_len, seq):
    c = GLA_CHUNK
    n_all, n_ctx = seq // c, ctx_len // c
    h = pl.program_id(1)
    lane = lax.broadcasted_iota(jnp.int32, (c, LANES), 1)
    is_q = lane < GDN_DK
    eye = (lax.broadcasted_iota(jnp.int32, (c, c), 0) == lax.broadcasted_iota(jnp.int32, (c, c), 1)).astype(F32)

    def prep(ci, carry):
        r0 = ci * c
        rows = pl.ds(pl.multiple_of(r0, c), c)
        qk = _silu(_conv_rows(qk_ref, r0, c, seq, ctx_len, cw_qk_ref[...], GDN_CONV // 2))
        va = _silu(_conv_rows(v_ref, r0, c, seq, ctx_len, cw_v_ref[...], GDN_CONV // 2))
        sq = qk * qk
        s_q = jnp.sum(jnp.where(is_q, sq, 0.0), axis=-1, keepdims=True)
        s_k = jnp.sum(jnp.where(is_q, 0.0, sq), axis=-1, keepdims=True)
        inv = lax.rsqrt(jnp.where(is_q, s_q, s_k) + NORM_EPS)
        qkn_ref[rows, :] = qk * inv * jnp.where(is_q, GDN_DK ** -0.5, 1.0)
        va_ref[rows, :] = va
        return carry

    lax.fori_loop(0, n_all, prep, 0)

    def local(ci, carry):
        rows = pl.ds(pl.multiple_of(ci * c, c), c)
        qkn = qkn_ref[rows, :]
        qn = qkn[:, :GDN_DK]
        kn = qkn[:, GDN_DK:]
        qb = qn.astype(BF16)
        kb = kn.astype(BF16)
        kk = _dot_nt(kb, kb)
        qk = _dot_nt(qb, kb)
        va = va_ref[rows, :]
        misc = misc_ref[0, rows, :].astype(F32)
        for d in range(2):
            a_col = jnp.sum(jnp.where(lane == d * GDN_HEADS + h, misc, 0.0), axis=-1, keepdims=True)
            b_col = jnp.sum(jnp.where(lane == 2 * GDN_HEADS + d * GDN_HEADS + h, misc, 0.0),
                            axis=-1, keepdims=True)
            a_log = par_ref[0, 2 * d:2 * d + 1, :]
            dt_b = par_ref[0, 2 * d + 1:2 * d + 2, :]
            loga = -jnp.exp(a_log) * jax.nn.softplus(a_col + dt_b)
            beta = jax.nn.sigmoid(b_col)
            r = _dot(a3_ref[d], jnp.concatenate(_split3(loga), axis=0))
            cum, rest, total = r[:c], r[c:2 * c], r[2 * c:2 * c + 8]
            rel = cum[:, :c] - cum.T[:c, :]
            dec = jnp.exp(jnp.minimum(rel, 0.0))
            n_mat = beta * (tri_ref[d, 1] * dec) * kk
            x = eye
            for l in range(lv_ref.shape[1]):
                nm = (n_mat * lv_ref[d, l]).astype(BF16)
                x = x - _dot(_dot(x.astype(BF16), nm).astype(BF16), x.astype(BF16))
            xb = x.astype(BF16)
            ecum = jnp.exp(cum[:, :GDN_DK])
            w_ref[d, rows, :] = _dot(xb, (beta * ecum * kn).astype(BF16)).astype(BF16)
            u0_ref[d, rows, :] = _dot(xb, (beta * va).astype(BF16))
            qkd_ref[d, rows, :] = (qk * (tri_ref[d, 0] * dec)).astype(BF16)
            qt_ref[d, rows, :] = (qn * ecum).astype(BF16)
            ktt_ref[d, ci] = (kn * jnp.exp(rest[:, :GDN_DK])).T.astype(BF16)
            el_ref[d, ci] = jnp.exp(total)
        return carry

    lax.fori_loop(0, n_all, local, 0)

    def one(d, ci, s):
        rows = pl.ds(pl.multiple_of(ci * c, c), c)
        sb = s.astype(BF16)
        u = u0_ref[d, rows, :] - _dot(w_ref[d, rows, :], sb)
        ub = u.astype(BF16)
        o = _dot(qt_ref[d, rows, :], sb) + _dot(qkd_ref[d, rows, :], ub)
        s = el_ref[d, ci][0:1] * s + _dot(ktt_ref[d, ci], ub)
        return o, s, rows

    def step(it, carry):
        s_f, s_b = carry
        cf, cb = _chunk_order(it, n_ctx, n_all)
        o_f, s_f, rows_f = one(0, cf, s_f)
        acc_ref[0, rows_f, :] = o_f
        o_b, s_b, rows_b = one(1, cb, s_b)
        acc_ref[1, rows_b, :] = o_b
        return s_f, s_b

    zero = jnp.zeros((GDN_DK, GDN_DV), F32)
    lax.fori_loop(0, n_all, step, (zero, zero))

    def readout(ci, carry):
        rows = pl.ds(pl.multiple_of(ci * c, c), c)
        o = acc_ref[0, rows, :] + acc_ref[1, rows, :]
        o = o * lax.rsqrt(jnp.mean(o * o, axis=-1, keepdims=True) + NORM_EPS)
        o_ref[0, rows, :] = (o * nw_ref[...] * _silu(g_ref[0, rows, :].astype(F32))).astype(o_ref.dtype)
        return carry

    lax.fori_loop(0, n_all, readout, 0)


def _gdn(z, conv_w, par, norm_w, ctx_len):
    b, s, _ = z.shape
    c = GLA_CHUNK
    n = s // c
    a3, lv, tri = _gdn_consts(c)
    a3 = jnp.asarray(a3, BF16)
    lv = jnp.asarray(lv, F32)
    tri = jnp.asarray(tri, F32)
    kern = functools.partial(_gdn_kernel, ctx_len=ctx_len, seq=s)

    def col(blk):
        return pl.BlockSpec((1, s, LANES), lambda i, h: (i, 0, blk + h))

    def full(a):
        nd = a.ndim
        return pl.BlockSpec(a.shape, lambda i, h: (0,) * nd)

    return pl.pallas_call(
        kern,
        grid=(b, GDN_HEADS),
        in_specs=[col(BLK_GDN_QK), col(BLK_GDN_V), col(BLK_GDN_G),
                  pl.BlockSpec((1, s, LANES), lambda i, h: (i, 0, BLK_GDN_MISC)),
                  pl.BlockSpec((GDN_CONV, LANES), lambda i, h: (0, h)),
                  pl.BlockSpec((GDN_CONV, LANES), lambda i, h: (0, GDN_HEADS + h)),
                  pl.BlockSpec((1, 8, LANES), lambda i, h: (h, 0, 0)),
                  pl.BlockSpec((1, LANES), lambda i, h: (0, 0)),
                  full(a3), full(lv), full(tri)],
        out_specs=pl.BlockSpec((1, s, LANES), lambda i, h: (i, 0, h)),
        out_shape=jax.ShapeDtypeStruct((b, s, GDN_HEADS * GDN_DV), BF16),
        scratch_shapes=[pltpu.VMEM((s, LANES), F32),
                        pltpu.VMEM((s, LANES), F32),
                        pltpu.VMEM((2, s, LANES), F32),
                        pltpu.VMEM((2, s, GDN_DK), BF16),
                        pltpu.VMEM((2, s, LANES), F32),
                        pltpu.VMEM((2, s, c), BF16),
                        pltpu.VMEM((2, s, GDN_DK), BF16),
                        pltpu.VMEM((2, n, GDN_DK, c), BF16),
                        pltpu.VMEM((2, n, 8, LANES), F32)],
        compiler_params=pltpu.CompilerParams(
            dimension_semantics=("parallel", "parallel"), vmem_limit_bytes=VMEM_LIMIT),
        name="gated_deltanet",
    )(z, z, z, z, conv_w, conv_w, par, norm_w, a3, lv, tri)


def _lru_kernel(x_ref, gate_ref, cw_ref, cb_ref, wd_ref, bd_ref, lam_ref, o_ref,
                af_ref, uf_ref, ab_ref, ub_ref, hf_ref, hb_ref, *, ctx_len, seq, c):
    w = LRU_WIDTH
    n_all, n_ctx = seq // c, ctx_len // c

    def gates(ci, d, a_ref, u_ref):
        r0 = ci * c
        xc = _conv_rows(x_ref, r0, c, seq, ctx_len, cw_ref[...], LRU_CONV // 2) + cb_ref[...]
        y = _dot(xc.astype(BF16), wd_ref[d]) + bd_ref[d]
        r = jax.nn.sigmoid(y[:, :w])
        i = jax.nn.sigmoid(y[:, w:])
        log_a = -LRU_C * r * jax.nn.softplus(-lam_ref[d])
        a = jnp.exp(log_a)
        a_ref[...] = a
        u_ref[...] = jnp.sqrt(1.0 - a * a) * (i * xc)

    def step(it, carry):
        h_f, h_b = carry
        cf, cb = _chunk_order(it, n_ctx, n_all)
        gates(cf, 0, af_ref, uf_ref)
        gates(cb, 1, ab_ref, ub_ref)
        base_f = cf * c
        base_b = cb * c

        def row(t, hh):
            hf, hb = hh
            tb = c - 1 - t
            hf = af_ref[pl.ds(t, 1), :] * hf + uf_ref[pl.ds(t, 1), :]
            hb = ab_ref[pl.ds(tb, 1), :] * hb + ub_ref[pl.ds(tb, 1), :]
            hf_ref[pl.ds(base_f + t, 1), :] = hf
            hb_ref[pl.ds(base_b + tb, 1), :] = hb
            return hf, hb

        return lax.fori_loop(0, c, row, (h_f, h_b), unroll=8)

    zero = jnp.zeros((1, w), F32)
    lax.fori_loop(0, n_all, step, (zero, zero))

    def readout(ci, carry):
        rows = pl.ds(pl.multiple_of(ci * c, c), c)
        hsum = hf_ref[rows, :] + hb_ref[rows, :]
        o_ref[0, rows, :] = (jax.nn.gelu(gate_ref[0, rows, :].astype(F32)) * hsum).astype(o_ref.dtype)
        return carry

    lax.fori_loop(0, n_all, readout, 0)


def _rglru(z, conv_w, conv_b, w_dense, b_dense, lam, ctx_len):
    b, s, _ = z.shape
    w = LRU_WIDTH
    c = _pick_tile(math.gcd(ctx_len, s), (LRU_ROWS, 128, 64))
    kern = functools.partial(_lru_kernel, ctx_len=ctx_len, seq=s, c=c)

    def full(a):
        nd = a.ndim
        return pl.BlockSpec(a.shape, lambda i: (0,) * nd)

    return pl.pallas_call(
        kern,
        grid=(b,),
        in_specs=[pl.BlockSpec((1, s, w), lambda i: (i, 0, BLK_LRU_X * LANES // w)),
                  pl.BlockSpec((1, s, w), lambda i: (i, 0, BLK_LRU_G * LANES // w)),
                  full(conv_w), full(conv_b), full(w_dense), full(b_dense), full(lam)],
        out_specs=pl.BlockSpec((1, s, w), lambda i: (i, 0, 0)),
        out_shape=jax.ShapeDtypeStruct((b, s, w), BF16),
        scratch_shapes=[pltpu.VMEM((c, w), F32)] * 4 + [pltpu.VMEM((s, w), F32)] * 2,
        compiler_params=pltpu.CompilerParams(
            dimension_semantics=("parallel",), vmem_limit_bytes=VMEM_LIMIT),
        name="rglru",
    )(z, z, conv_w, conv_b, w_dense, b_dense, lam)


def _merge_kernel(x_ref, b0_ref, b1_ref, b2_ref, b3_ref, gate_ref, gl_ref, gc_ref, wb_ref, wo_ref,
                  lng_ref, lnb_ref, o_ref, *, ctx_len, tm, alpha):
    t = pl.program_id(1)
    merged = None
    for j, br in enumerate((b0_ref, b1_ref, b2_ref, b3_ref)):
        gate = jax.nn.sigmoid(gate_ref[0, :, j * D_MODEL:(j + 1) * D_MODEL].astype(F32))
        term = gate * _dot(br[0], wb_ref[j])
        merged = term if merged is None else merged + term
    y = _dot(merged.astype(BF16), wo_ref[...])
    rows = _row_ids((tm, 1), t * tm)
    g1 = jnp.where(rows < ctx_len, gc_ref[...], gl_ref[0])
    o_ref[0] = _ln0(alpha * x_ref[0] + g1 * y) * lng_ref[...] + lnb_ref[...]


def _merge(h, branches, z, g_l, g_c, w_branch, w_out, ln_g, ln_b, ctx_len, alpha):
    b, s, d = h.shape
    tm = _pick_tile(s, (256, 128))
    kern = functools.partial(_merge_kernel, ctx_len=ctx_len, tm=tm, alpha=alpha)
    bw = BRANCH_WIDTH

    def tile(width, blk=0):
        return pl.BlockSpec((1, tm, width), lambda i, t: (i, t, blk))

    def full(a):
        nd = a.ndim
        return pl.BlockSpec(a.shape, lambda i, t: (0,) * nd)

    return pl.pallas_call(
        kern,
        grid=(b, s // tm),
        in_specs=[tile(d), tile(bw), tile(bw), tile(bw), tile(bw), tile(N_BRANCH * d, BLK_MG),
                  pl.BlockSpec((1, 1, d), lambda i, t: (i, 0, 0)), full(g_c),
                  full(w_branch), full(w_out), full(ln_g), full(ln_b)],
        out_specs=tile(d),
        out_shape=jax.ShapeDtypeStruct((b, s, d), F32),
        compiler_params=pltpu.CompilerParams(
            dimension_semantics=("parallel", "parallel"), vmem_limit_bytes=VMEM_LIMIT),
        name="merge",
    )(h, *branches, z, g_l, g_c, w_branch, w_out, ln_g, ln_b)


def _ffn_kernel(x_ref, xp_ref, xn_ref, shl_ref, scl_ref, gl_ref, shc_ref, scc_ref, gc_ref,
                wv_ref, wg_ref, cwv_ref, cwg_ref, cbv_ref, cbg_ref, wd_ref, lng_ref, lnb_ref,
                o_ref, xm_ref, xh_ref, acc_ref, *, ctx_len, seq, tm, alpha):
    t = pl.program_id(1)
    j = pl.program_id(2)
    r0 = t * tm

    @pl.when(j == 0)
    def _():
        rows = _row_ids((tm, 1), r0)
        xm = _modulated(x_ref[0], rows, ctx_len, shl_ref[0], scl_ref[0], shc_ref[...], scc_ref[...])
        xm_ref[...] = xm.astype(BF16)
        hrows = jnp.concatenate([_row_ids((8, 1), r0 - 8), _row_ids((8, 1), r0 + tm)], axis=0)
        xh = jnp.concatenate([xp_ref[0], xn_ref[0]], axis=0)
        xhm = _modulated(xh, hrows, ctx_len, shl_ref[0], scl_ref[0], shc_ref[...], scc_ref[...])
        xh_ref[...] = xhm.astype(BF16)
        acc_ref[...] = jnp.zeros_like(acc_ref)

    rows = _row_ids((tm, 1), r0)
    seg_first = (rows == 0) | (rows == ctx_len)
    seg_last = (rows == ctx_len - 1) | (rows == seq - 1)
    first = rows == r0
    last = rows == r0 + tm - 1

    def conv(w_ref, cw_ref, cb_ref):
        u = _dot(xm_ref[...], w_ref[...])
        uh = _dot(xh_ref[...], w_ref[...])
        up = jnp.where(first, uh[7:8], pltpu.roll(u, 1, 0))
        un = jnp.where(last, uh[8:9], pltpu.roll(u, tm - 1, 0))
        up = jnp.where(seg_first, 0.0, up)
        un = jnp.where(seg_last, 0.0, un)
        cw = cw_ref[...]
        return cw[0:1] * up + cw[1:2] * u + cw[2:3] * un + cb_ref[...]

    val = conv(wv_ref, cwv_ref, cbv_ref)
    gate = conv(wg_ref, cwg_ref, cbg_ref)
    act = (_silu(gate) * val).astype(BF16)
    acc_ref[...] += _dot(act, wd_ref[...])

    @pl.when(j == pl.num_programs(2) - 1)
    def _():
        g2 = jnp.where(rows < ctx_len, gc_ref[...], gl_ref[0])
        o_ref[0] = _ln0(alpha * x_ref[0] + g2 * acc_ref[...]) * lng_ref[...] + lnb_ref[...]


def _conv_ffn(h, mods_l, mods_c, w_up, conv_w, conv_b, w_down, ln_g, ln_b, ctx_len, alpha):
    b, s, d = h.shape
    d_ff = w_down.shape[0]
    tm = _pick_tile(s, (768, 384, 256, 128))
    fc = _pick_tile(d_ff, (256, 128))
    nj = d_ff // fc
    kern = functools.partial(_ffn_kernel, ctx_len=ctx_len, seq=s, tm=tm, alpha=alpha)
    nb8 = s // 8

    def vec_l():
        return pl.BlockSpec((1, 1, d), lambda i, t, j: (i, 0, 0))

    def vec_c():
        return pl.BlockSpec((1, d), lambda i, t, j: (0, 0))

    sh_l, sc_l, g_l = mods_l
    sh_c, sc_c, g_c = mods_c
    return pl.pallas_call(
        kern,
        grid=(b, s // tm, nj),
        in_specs=[pl.BlockSpec((1, tm, d), lambda i, t, j: (i, t, 0)),
                  pl.BlockSpec((1, 8, d), lambda i, t, j: (i, jnp.maximum(t * (tm // 8) - 1, 0), 0)),
                  pl.BlockSpec((1, 8, d), lambda i, t, j: (i, jnp.minimum((t + 1) * (tm // 8), nb8 - 1), 0)),
                  vec_l(), vec_l(), vec_l(), vec_c(), vec_c(), vec_c(),
                  pl.BlockSpec((d, fc), lambda i, t, j: (0, j)),
                  pl.BlockSpec((d, fc), lambda i, t, j: (0, nj + j)),
                  pl.BlockSpec((FFN_CONV, fc), lambda i, t, j: (0, j)),
                  pl.BlockSpec((FFN_CONV, fc), lambda i, t, j: (0, nj + j)),
                  pl.BlockSpec((1, fc), lambda i, t, j: (0, j)),
                  pl.BlockSpec((1, fc), lambda i, t, j: (0, nj + j)),
                  pl.BlockSpec((fc, d), lambda i, t, j: (j, 0)),
                  pl.BlockSpec((1, d), lambda i, t, j: (0, 0)),
                  pl.BlockSpec((1, d), lambda i, t, j: (0, 0))],
        out_specs=pl.BlockSpec((1, tm, d), lambda i, t, j: (i, t, 0)),
        out_shape=jax.ShapeDtypeStruct((b, s, d), F32),
        scratch_shapes=[pltpu.VMEM((tm, d), BF16), pltpu.VMEM((16, d), BF16), pltpu.VMEM((tm, d), F32)],
        compiler_params=pltpu.CompilerParams(
            dimension_semantics=("parallel", "parallel", "arbitrary"), vmem_limit_bytes=VMEM_LIMIT),
        name="conv_ffn",
    )(h, h, h, sh_l, sc_l, g_l, sh_c, sc_c, g_c, w_up, w_up, conv_w, conv_w, conv_b, conv_b,
      w_down, ln_g, ln_b)


def _rope_tables(seq, ctx_len):
    quarter = RET_DK // 4
    inv_freq = ROPE_BASE ** (-jnp.arange(quarter, dtype=F32) / quarter)
    p = jnp.arange(seq - ctx_len)
    row = (p // GRID_W).astype(F32)
    col = (p % GRID_W).astype(F32)
    ang_r = row[:, None] * inv_freq[None, :]
    ang_c = col[:, None] * inv_freq[None, :]
    cos = jnp.concatenate([jnp.cos(ang_r)] * 2 + [jnp.cos(ang_c)] * 2, axis=-1)
    sin = jnp.concatenate([-jnp.sin(ang_r), jnp.sin(ang_r), -jnp.sin(ang_c), jnp.sin(ang_c)], axis=-1)
    cos = jnp.concatenate([jnp.ones((ctx_len, RET_DK), F32), cos], axis=0)
    sin = jnp.concatenate([jnp.zeros((ctx_len, RET_DK), F32), sin], axis=0)
    ks = RET_DK ** -0.5
    return jnp.concatenate([cos, cos * ks], axis=-1), jnp.concatenate([sin, sin * ks], axis=-1)


def _block_diag(w):
    nb, k, _ = w.shape
    eye = jnp.eye(nb, dtype=w.dtype)
    return (eye[:, None, :, None] * w[:, :, None, :]).reshape(nb * k, nb * k)


def kernel(x, c, ctx, c_ctx, w_mod, b_mod, w_in, hg_lb, hg_norm, gdn_conv, gdn_a_log, gdn_dt_bias,
           gdn_norm, lru_conv, lru_conv_b, lru_w_a, lru_b_a, lru_w_i, lru_b_i, lru_lam, w_branch,
           w_out, ln_mix_g, ln_mix_b, w_up, ffn_conv, ffn_conv_b, w_down, ln_ffn_g, ln_ffn_b):
    depth = w_in.shape[0]
    batch, seq_l, d = x.shape
    ctx_len = ctx.shape[1]
    seq = ctx_len + seq_l
    alpha = (2.0 * depth) ** 0.25

    perm, n_in = _proj_perm()
    w_in_p = jnp.take(jnp.concatenate([w_in.astype(BF16), jnp.zeros((depth, d, 1), BF16)], axis=-1),
                      jnp.asarray(perm), axis=-1)
    gdn_conv_p = jnp.take(gdn_conv, jnp.asarray(_gdn_conv_perm()), axis=-1)
    lb = jnp.cumsum(jax.nn.softmax(hg_lb.astype(F32), axis=0), axis=0)
    lb = lb - lb[:1]
    log_gamma = jnp.log1p(-jnp.exp2(-5.0 - jnp.arange(RET_HEADS, dtype=F32)))
    lg_rows = jnp.broadcast_to(log_gamma[:, None, None], (RET_HEADS, 1, LANES))
    cos_t, sin_t = _rope_tables(seq, ctx_len)
    w_branch_b = w_branch.astype(BF16)
    w_out_b = w_out.astype(BF16)
    w_up_b = w_up.astype(BF16)
    w_down_b = w_down.astype(BF16)

    cc = jnp.concatenate([c, c_ctx[None, :], jnp.zeros((7, d), F32)], axis=0)
    mod = _modulation(cc, w_mod, b_mod)

    h = jnp.concatenate([ctx, x], axis=1)
    for i in range(depth):
        ml = mod[i, :batch].reshape(batch, N_MOD, 1, d)
        mc = mod[i, batch].reshape(N_MOD, 1, d)
        sh1, sc1, g1, sh2, sc2, g2 = (ml[:, k] for k in range(N_MOD))
        csh1, csc1, cg1, csh2, csc2, cg2 = (mc[k] for k in range(N_MOD))

        z = _in_projection(h, sh1, sc1, csh1, csc1, w_in_p[i], ctx_len)

        ret = _retention(z, lg_rows, cos_t, sin_t, ctx_len)
        hg = _hgrn2(z, lb[i, 0][None, :], lb[i, 1][None, :], hg_norm[i][None, :], ctx_len)
        par = jnp.stack([gdn_a_log[i, 0], gdn_dt_bias[i, 0], gdn_a_log[i, 1], gdn_dt_bias[i, 1]], axis=1)
        par = jnp.broadcast_to(jnp.pad(par, ((0, 0), (0, 4)))[:, :, None], (GDN_HEADS, 8, LANES))
        gd = _gdn(z, gdn_conv_p[i], par, gdn_norm[i][None, :], ctx_len)
        w_dense = jnp.stack([
            jnp.concatenate([_block_diag(lru_w_a[i, dd]), _block_diag(lru_w_i[i, dd])], axis=1)
            for dd in range(2)]).astype(BF16)
        b_dense = jnp.stack([jnp.concatenate([lru_b_a[i, dd], lru_b_i[i, dd]])[None, :] for dd in range(2)])
        lr = _rglru(z, lru_conv[i], lru_conv_b[i][None, :], w_dense, b_dense,
                    lru_lam[i].astype(F32)[:, None, :], ctx_len)

        h = _merge(h, (ret, hg, gd, lr), z, g1, cg1, w_branch_b[i], w_out_b[i],
                   ln_mix_g[i][None, :], ln_mix_b[i][None, :], ctx_len, alpha)
        h = _conv_ffn(h, (sh2, sc2, g2), (csh2, csc2, cg2), w_up_b[i], ffn_conv[i],
                      ffn_conv_b[i][None, :], w_down_b[i], ln_ffn_g[i][None, :], ln_ffn_b[i][None, :],
                      ctx_len, alpha)
    return h[:, ctx_len:]
```

```python
import functools
import math

import numpy as np
import jax
import jax.numpy as jnp
from jax import lax
from jax.experimental import pallas as pl
from jax.experimental.pallas import tpu as pltpu

F32 = jnp.float32
BF16 = jnp.bfloat16

D_MODEL = 1024
GRID_W = 64
NORM_EPS = 1e-6
ROPE_BASE = 10000.0
RET_HEADS, RET_DK, RET_DV = 4, 64, 128
HG_HEADS, HG_DK, HG_DV = 4, 128, 128
GDN_HEADS, GDN_DK, GDN_DV, GDN_CONV = 4, 64, 128, 4
LRU_WIDTH, LRU_BLOCKS, LRU_CONV, LRU_C = 512, 8, 4, 8.0
N_BRANCH, BRANCH_WIDTH = 4, 512
D_FF, FFN_CONV = 2816, 3
N_MOD = 6

LANES = 128
HALO = 16
VMEM_LIMIT = 56 * 1024 * 1024

BLK_MG = 0
BLK_LRU_X = 32
BLK_LRU_G = 36
BLK_HG_Q = 40
BLK_HG_FF = 44
BLK_HG_FB = 48
BLK_HG_I = 52
BLK_HG_G = 56
BLK_RET_QK = 60
BLK_RET_V = 64
BLK_RET_G = 68
BLK_GDN_QK = 72
BLK_GDN_V = 76
BLK_GDN_G = 80
BLK_GDN_MISC = 84
N_BLK = 85
N_PROJ = N_BLK * LANES

RET_CHUNK = 128
GLA_CHUNK = 64
LRU_ROWS = 256
GDN_GROUP = 4
HG_GROUP = 2


def _dot(a, b):
    return jnp.dot(a, b, preferred_element_type=F32)


def _dot_nt(a, b):
    return lax.dot_general(a, b, (((1,), (1,)), ((), ())), preferred_element_type=F32)


def _dot_tn(a, b):
    return lax.dot_general(a, b, (((0,), (0,)), ((), ())), preferred_element_type=F32)


def _silu(x):
    return x * jax.nn.sigmoid(x)


def _split3(x):
    hi = x.astype(BF16)
    r1 = x - hi.astype(F32)
    mid = r1.astype(BF16)
    lo = (r1 - mid.astype(F32)).astype(BF16)
    return hi, mid, lo


def _ln0(x):
    mu = jnp.mean(x, axis=-1, keepdims=True)
    xc = x - mu
    var = jnp.mean(xc * xc, axis=-1, keepdims=True)
    return xc * lax.rsqrt(var + NORM_EPS)


def _row_ids(shape, start):
    return start + lax.broadcasted_iota(jnp.int32, shape, 0)


def _chunk_order(it, n_ctx, n_all):
    cb = jnp.where(it < n_ctx, n_ctx - 1 - it, n_all + n_ctx - 1 - it)
    return it, cb


def _level_structs(c):
    t = np.arange(c)
    j = t[None, :]
    segs, masks = [], []
    half = c // 2
    while half >= 1:
        blk = t // (2 * half)
        ref = blk * 2 * half + half - 1
        upper = (t - blk * 2 * half) >= half
        seg_u = (j > ref[:, None]) & (j <= t[:, None])
        seg_l = (j > t[:, None]) & (j <= ref[:, None])
        segs.append(np.where(upper[:, None], seg_u, seg_l))
        masks.append((blk[:, None] == blk[None, :]) & upper[:, None] & (~upper)[None, :])
        half //= 2
    return segs, masks


def _cum_rows(c):
    t = np.arange(c)
    j = t[None, :]
    incl = j <= t[:, None]
    rest = j > t[:, None]
    ones = np.ones((8, c), bool)
    return [incl, rest, ones]


def _flip2(m):
    return m[::-1, ::-1]


@functools.lru_cache(maxsize=None)
def _hg_consts(c):
    segs, masks = _level_structs(c)
    rows_f = segs + _cum_rows(c)
    rows_b = [_flip2(m) for m in segs] + [_flip2(m) for m in _cum_rows(c)[:2]] + _cum_rows(c)[2:]
    masks = masks + [np.eye(c, dtype=bool)]
    a_f = np.concatenate(rows_f, axis=0).astype(np.float32)
    a_b = np.concatenate(rows_b, axis=0).astype(np.float32)
    a3 = np.stack([np.tile(a_f, (1, 3)), np.tile(a_b, (1, 3))])
    m_f = np.stack(masks).astype(np.float32)
    m_b = np.stack([_flip2(m) for m in masks]).astype(np.float32)
    return a3, np.stack([m_f, m_b])


@functools.lru_cache(maxsize=None)
def _gdn_consts(c):
    _, masks = _level_structs(c)
    masks = masks[::-1]
    t = np.arange(c)
    incl = t[None, :] <= t[:, None]

    def bd4(m_f):
        out = np.zeros((4 * c, 4 * c), np.float32)
        for b in range(4):
            out[b * c:(b + 1) * c, b * c:(b + 1) * c] = m_f if b % 2 == 0 else _flip2(m_f)
        return out

    ones = np.ones((c, c), bool)
    a = np.concatenate([bd4(incl), bd4(ones)], axis=0)
    a3 = np.tile(a, (1, 3))
    lv = np.stack([bd4(m) for m in masks])
    return a3, lv, bd4(incl)


def _proj_perm():
    names = (('ret_q', 256), ('ret_k', 256), ('ret_v', 512), ('ret_g', 512),
             ('hg_q', 512), ('hg_f_fwd', 512), ('hg_f_bwd', 512), ('hg_i', 512), ('hg_g', 512),
             ('gdn_qkv', 1024), ('gdn_a', 8), ('gdn_b', 8), ('gdn_g', 512),
             ('lru_x', 512), ('lru_gate', 512), ('merge_gate', 4096))
    off, o = {}, 0
    for name, w in names:
        off[name] = o
        o += w
    n_in = o
    perm = np.full((N_PROJ,), n_in, np.int64)

    def put(blk, src, width):
        perm[blk * LANES: blk * LANES + width] = np.arange(src, src + width)

    put(BLK_MG, off['merge_gate'], 4096)
    put(BLK_LRU_X, off['lru_x'], 512)
    put(BLK_LRU_G, off['lru_gate'], 512)
    put(BLK_HG_Q, off['hg_q'], 512)
    put(BLK_HG_FF, off['hg_f_fwd'], 512)
    put(BLK_HG_FB, off['hg_f_bwd'], 512)
    put(BLK_HG_I, off['hg_i'], 512)
    put(BLK_HG_G, off['hg_g'], 512)
    for h in range(RET_HEADS):
        perm[(BLK_RET_QK + h) * LANES: (BLK_RET_QK + h) * LANES + 64] = off['ret_q'] + 64 * h + np.arange(64)
        perm[(BLK_RET_QK + h) * LANES + 64: (BLK_RET_QK + h + 1) * LANES] = off['ret_k'] + 64 * h + np.arange(64)
    put(BLK_RET_V, off['ret_v'], 512)
    put(BLK_RET_G, off['ret_g'], 512)
    gq = off['gdn_qkv']
    for h in range(GDN_HEADS):
        perm[(BLK_GDN_QK + h) * LANES: (BLK_GDN_QK + h) * LANES + 64] = gq + 64 * h + np.arange(64)
        perm[(BLK_GDN_QK + h) * LANES + 64: (BLK_GDN_QK + h + 1) * LANES] = gq + 256 + 64 * h + np.arange(64)
    put(BLK_GDN_V, gq + 512, 512)
    put(BLK_GDN_G, off['gdn_g'], 512)
    put(BLK_GDN_MISC, off['gdn_a'], 8)
    perm[BLK_GDN_MISC * LANES + 8: BLK_GDN_MISC * LANES + 16] = off['gdn_b'] + np.arange(8)
    return perm, n_in


def _take_columns(w, perm, n_src):
    pieces, start = [], 0
    for i in range(1, len(perm) + 1):
        pad = perm[start] == n_src
        if i < len(perm) and ((pad and perm[i] == n_src) or
                              (not pad and perm[i] != n_src and perm[i] == perm[i - 1] + 1)):
            continue
        if pad:
            pieces.append(jnp.zeros(w.shape[:-1] + (i - start,), w.dtype))
        else:
            pieces.append(w[..., int(perm[start]):int(perm[start]) + i - start])
        start = i
    return jnp.concatenate(pieces, axis=-1)


def _gdn_conv_perm():
    p = np.zeros((1024,), np.int64)
    for h in range(GDN_HEADS):
        p[128 * h: 128 * h + 64] = 64 * h + np.arange(64)
        p[128 * h + 64: 128 * h + 128] = 256 + 64 * h + np.arange(64)
    p[512:] = 512 + np.arange(512)
    return p


def _mod_kernel(c_ref, w_ref, b_ref, o_ref):
    s = _silu(c_ref[...])
    o_ref[0] = jnp.dot(s, w_ref[0], preferred_element_type=F32,
                       precision=lax.Precision.HIGHEST) + b_ref[0]


def _modulation(cc, w_mod, b_mod):
    depth, d, n = w_mod.shape
    rows = cc.shape[0]
    tn = 1024
    return pl.pallas_call(
        _mod_kernel,
        grid=(depth, n // tn),
        in_specs=[pl.BlockSpec((rows, d), lambda l, j: (0, 0)),
                  pl.BlockSpec((1, d, tn), lambda l, j: (l, 0, j)),
                  pl.BlockSpec((1, 1, tn), lambda l, j: (l, 0, j))],
        out_specs=pl.BlockSpec((1, rows, tn), lambda l, j: (l, 0, j)),
        out_shape=jax.ShapeDtypeStruct((depth, rows, n), F32),
        compiler_params=pltpu.CompilerParams(vmem_limit_bytes=VMEM_LIMIT),
        name="modulation",
    )(cc, w_mod, b_mod.reshape(depth, 1, n))


def _modulated(x, rows, ctx_len, sh_l, sc_l, sh_c, sc_c):
    is_ctx = rows < ctx_len
    scale = jnp.where(is_ctx, sc_c, sc_l)
    shift = jnp.where(is_ctx, sh_c, sh_l)
    return _ln0(x) * (1.0 + scale) + shift


def _inproj_kernel(x_ref, shl_ref, scl_ref, shc_ref, scc_ref, w_ref, o_ref, xm_ref, *, ctx_len, tm):
    t = pl.program_id(1)

    @pl.when(pl.program_id(2) == 0)
    def _():
        rows = _row_ids((tm, 1), t * tm)
        xm = _modulated(x_ref[0], rows, ctx_len, shl_ref[0], scl_ref[0], shc_ref[...], scc_ref[...])
        xm_ref[...] = xm.astype(BF16)

    o_ref[0] = _dot(xm_ref[...], w_ref[...]).astype(o_ref.dtype)


def _pick_tile(n, cands):
    for c in cands:
        if n % c == 0:
            return c
    return n


def _in_projection(h, sh_l, sc_l, sh_c, sc_c, w_p, ctx_len):
    b, s, d = h.shape
    n = w_p.shape[1]
    tm = _pick_tile(s, (1152, 768, 512, 384, 256, 128))
    tn = _pick_tile(n, (2176, 1280, 640, 128))
    kern = functools.partial(_inproj_kernel, ctx_len=ctx_len, tm=tm)
    return pl.pallas_call(
        kern,
        grid=(b, s // tm, n // tn),
        in_specs=[pl.BlockSpec((1, tm, d), lambda i, t, j: (i, t, 0)),
                  pl.BlockSpec((1, 1, d), lambda i, t, j: (i, 0, 0)),
                  pl.BlockSpec((1, 1, d), lambda i, t, j: (i, 0, 0)),
                  pl.BlockSpec((1, d), lambda i, t, j: (0, 0)),
                  pl.BlockSpec((1, d), lambda i, t, j: (0, 0)),
                  pl.BlockSpec((d, tn), lambda i, t, j: (0, j))],
        out_specs=pl.BlockSpec((1, tm, tn), lambda i, t, j: (i, t, j)),
        out_shape=jax.ShapeDtypeStruct((b, s, n), BF16),
        scratch_shapes=[pltpu.VMEM((tm, d), BF16)],
        compiler_params=pltpu.CompilerParams(
            dimension_semantics=("parallel", "parallel", "arbitrary"),
            vmem_limit_bytes=VMEM_LIMIT),
        name="in_projection",
    )(h, sh_l, sc_l, sh_c, sc_c, w_p)


def _ret_kernel(lg_ref, qk_ref, v_ref, g_ref, cos_ref, sin_ref, o_ref, qkr_ref, of_ref, ob_ref,
                *, ctx_len, seq):
    c = RET_CHUNK
    n_all, n_ctx = seq // c, ctx_len // c
    lane = lax.broadcasted_iota(jnp.int32, (c, LANES), 1)
    lg = lg_ref[0]

    def rope(ci, carry):
        rows = pl.ds(pl.multiple_of(ci * c, c), c)
        x = qk_ref[0, rows, :].astype(F32)
        swapped = jnp.where(lane % 32 < 16, pltpu.roll(x, LANES - 16, 1), pltpu.roll(x, 16, 1))
        qkr_ref[rows, :] = x * cos_ref[rows, :] + swapped * sin_ref[rows, :]
        return carry

    lax.fori_loop(0, n_all, rope, 0)

    pos = lax.broadcasted_iota(jnp.int32, (c, LANES), 0).astype(F32)
    is_q = lane < RET_DK
    fac_f = jnp.exp(jnp.where(is_q, pos + 1.0, c - 1.0 - pos) * lg)
    fac_b = jnp.exp(jnp.where(is_q, c - pos, pos) * lg)
    rr = lax.broadcasted_iota(jnp.int32, (c, c), 0)
    cc = lax.broadcasted_iota(jnp.int32, (c, c), 1)
    dist = jnp.abs(rr - cc).astype(F32)
    dmat = jnp.exp(dist * lg[:, :c]) * jnp.where(rr == cc, 2.0, 1.0)
    gc = jnp.exp(lg * float(c))

    def step(it, carry):
        s_f, s_b = carry
        cf, cb = _chunk_order(it, n_ctx, n_all)
        rows_f = pl.ds(pl.multiple_of(cf * c, c), c)
        rows_b = pl.ds(pl.multiple_of(cb * c, c), c)

        qk = qkr_ref[rows_f, :]
        v = v_ref[0, rows_f, :]
        q = qk[:, :RET_DK].astype(BF16)
        k = qk[:, RET_DK:].astype(BF16)
        p = (_dot_nt(q, k) * dmat).astype(BF16)
        qkt = qk * fac_f
        o = _dot(p, v) + _dot(qkt[:, :RET_DK].astype(BF16), s_f.astype(BF16))
        s_f = gc * s_f + _dot_tn(qkt[:, RET_DK:].astype(BF16), v)
        of_ref[rows_f, :] = o

        qkb = qkr_ref[rows_b, :] * fac_b
        vb = v_ref[0, rows_b, :]
        ob_ref[rows_b, :] = _dot(qkb[:, :RET_DK].astype(BF16), s_b.astype(BF16))
        s_b = gc * s_b + _dot_tn(qkb[:, RET_DK:].astype(BF16), vb)
        return s_f, s_b

    zero = jnp.zeros((RET_DK, RET_DV), F32)
    lax.fori_loop(0, n_all, step, (zero, zero))

    def readout(ci, carry):
        rows = pl.ds(pl.multiple_of(ci * c, c), c)
        o = _ln0(of_ref[rows, :] + ob_ref[rows, :])
        o_ref[0, rows, :] = (o * _silu(g_ref[0, rows, :].astype(F32))).astype(o_ref.dtype)
        return carry

    lax.fori_loop(0, n_all, readout, 0)


def _retention(z, lg_rows, cos_t, sin_t, ctx_len):
    b, s, _ = z.shape
    kern = functools.partial(_ret_kernel, ctx_len=ctx_len, seq=s)

    def col(blk):
        return pl.BlockSpec((1, s, LANES), lambda i, h: (i, 0, blk + h))

    return pl.pallas_call(
        kern,
        grid=(b, RET_HEADS),
        in_specs=[pl.BlockSpec((1, 1, LANES), lambda i, h: (h, 0, 0)),
                  col(BLK_RET_QK), col(BLK_RET_V), col(BLK_RET_G),
                  pl.BlockSpec((s, LANES), lambda i, h: (0, 0)),
                  pl.BlockSpec((s, LANES), lambda i, h: (0, 0))],
        out_specs=pl.BlockSpec((1, s, LANES), lambda i, h: (i, 0, h)),
        out_shape=jax.ShapeDtypeStruct((b, s, RET_HEADS * RET_DV), BF16),
        scratch_shapes=[pltpu.VMEM((s, LANES), F32), pltpu.VMEM((s, LANES), F32),
                        pltpu.VMEM((s, LANES), F32)],
        compiler_params=pltpu.CompilerParams(
            dimension_semantics=("parallel", "parallel"), vmem_limit_bytes=VMEM_LIMIT),
        name="retention",
    )(lg_rows, z, z, z, cos_t, sin_t)


def _hg_local(q, logits, lb, v, a3_ref, m_ref, d):
    c = GLA_CHUNK
    nl = m_ref.shape[1] - 1
    f = lb + (1.0 - lb) * jax.nn.sigmoid(logits)
    g = jnp.log(f)
    kk = 1.0 - f
    g3 = jnp.concatenate(_split3(g), axis=0)
    r = _dot(a3_ref[d], g3)
    scores = m_ref[d, nl] * _dot_nt(q.astype(BF16), kk.astype(BF16))
    for l in range(nl):
        e = jnp.exp(r[l * c:(l + 1) * c])
        scores = scores + m_ref[d, l] * _dot_nt((q * e).astype(BF16), (kk * e).astype(BF16))
    o = _dot(scores.astype(BF16), v)
    cum = r[nl * c:(nl + 1) * c]
    rest = r[(nl + 1) * c:(nl + 2) * c]
    total = r[(nl + 2) * c:(nl + 2) * c + 8]
    return o, q * jnp.exp(cum), kk * jnp.exp(rest), jnp.exp(total)


def _hg_kernel(q_ref, ff_ref, fb_ref, i_ref, g_ref, lbf_ref, lbb_ref, nw_ref, a3_ref, m_ref, o_ref,
               acc_ref, qf_ref, qb_ref, kf_ref, kb_ref, df_ref, db_ref, *, ctx_len, seq):
    c = GLA_CHUNK
    n_all, n_ctx = seq // c, ctx_len // c

    def local_chunk(ci):
        rows = pl.ds(pl.multiple_of(ci * c, c), c)
        q = _silu(q_ref[0, rows, :].astype(F32))
        v = i_ref[0, rows, :]
        o_f, qt_f, kt_f, d_f = _hg_local(q, ff_ref[0, rows, :].astype(F32), lbf_ref[...], v,
                                         a3_ref, m_ref, 0)
        o_b, qt_b, kt_b, d_b = _hg_local(q, fb_ref[0, rows, :].astype(F32), lbb_ref[...], v,
                                         a3_ref, m_ref, 1)
        acc_ref[rows, :] = o_f + o_b
        qf_ref[rows, :] = qt_f.astype(BF16)
        qb_ref[rows, :] = qt_b.astype(BF16)
        kf_ref[rows, :] = kt_f.astype(BF16)
        kb_ref[rows, :] = kt_b.astype(BF16)
        df_ref[ci] = d_f
        db_ref[ci] = d_b

    group = _pick_tile(n_all, (HG_GROUP, 1))

    def local(gi, carry):
        for k in range(group):
            local_chunk(gi * group + k)
        return carry

    lax.fori_loop(0, n_all // group, local, 0)

    def step(it, carry):
        st_f, st_b = carry
        cf, cb = _chunk_order(it, n_ctx, n_all)
        rows_f = pl.ds(pl.multiple_of(cf * c, c), c)
        rows_b = pl.ds(pl.multiple_of(cb * c, c), c)
        acc_ref[rows_f, :] += _dot_nt(qf_ref[rows_f, :], st_f.astype(BF16))
        st_f = st_f * df_ref[cf][0:1] + _dot_tn(i_ref[0, rows_f, :], kf_ref[rows_f, :])
        acc_ref[rows_b, :] += _dot_nt(qb_ref[rows_b, :], st_b.astype(BF16))
        st_b = st_b * db_ref[cb][0:1] + _dot_tn(i_ref[0, rows_b, :], kb_ref[rows_b, :])
        return st_f, st_b

    zero = jnp.zeros((HG_DV, HG_DK), F32)
    lax.fori_loop(0, n_all, step, (zero, zero))

    def readout(ci, carry):
        rows = pl.ds(pl.multiple_of(ci * c, c), c)
        o = acc_ref[rows, :]
        o = o * lax.rsqrt(jnp.mean(o * o, axis=-1, keepdims=True) + NORM_EPS)
        o_ref[0, rows, :] = (o * nw_ref[...] * _silu(g_ref[0, rows, :].astype(F32))).astype(o_ref.dtype)
        return carry

    lax.fori_loop(0, n_all, readout, 0)


def _hgrn2(z, lb_f, lb_b, norm_w, ctx_len):
    b, s, _ = z.shape
    c = GLA_CHUNK
    a3, masks = _hg_consts(c)
    a3 = jnp.asarray(a3, BF16)
    masks = jnp.asarray(masks, F32)
    kern = functools.partial(_hg_kernel, ctx_len=ctx_len, seq=s)

    def col(blk):
        return pl.BlockSpec((1, s, LANES), lambda i, h: (i, 0, blk + h))

    def full(a):
        nd = a.ndim
        return pl.BlockSpec(a.shape, lambda i, h: (0,) * nd)

    return pl.pallas_call(
        kern,
        grid=(b, HG_HEADS),
        in_specs=[col(BLK_HG_Q), col(BLK_HG_FF), col(BLK_HG_FB), col(BLK_HG_I), col(BLK_HG_G),
                  pl.BlockSpec((1, LANES), lambda i, h: (0, h)),
                  pl.BlockSpec((1, LANES), lambda i, h: (0, h)),
                  pl.BlockSpec((1, LANES), lambda i, h: (0, 0)),
                  full(a3), full(masks)],
        out_specs=pl.BlockSpec((1, s, LANES), lambda i, h: (i, 0, h)),
        out_shape=jax.ShapeDtypeStruct((b, s, HG_HEADS * HG_DV), BF16),
        scratch_shapes=[pltpu.VMEM((s, LANES), F32)] + [pltpu.VMEM((s, LANES), BF16)] * 4
                       + [pltpu.VMEM((s // c, 8, LANES), F32)] * 2,
        compiler_params=pltpu.CompilerParams(
            dimension_semantics=("parallel", "parallel"), vmem_limit_bytes=VMEM_LIMIT),
        name="hgrn2",
    )(z, z, z, z, z, lb_f, lb_b, norm_w, a3, masks)


def _conv_rows(ref, r0, c, seq, ctx_len, w, taps_left):
    lo = pl.multiple_of(jnp.maximum(r0 - HALO, 0), HALO)
    hi = pl.multiple_of(jnp.minimum(r0 + c, seq - HALO), HALO)
    xh = jnp.concatenate([ref[0, pl.ds(lo, HALO), :].astype(F32),
                          ref[0, pl.ds(pl.multiple_of(r0, HALO), c), :].astype(F32),
                          ref[0, pl.ds(hi, HALO), :].astype(F32)], axis=0)
    rows = _row_ids((c + 2 * HALO, 1), r0 - HALO)
    seg_lo = jnp.where(r0 < ctx_len, 0, ctx_len)
    seg_hi = jnp.where(r0 < ctx_len, ctx_len, seq)
    xh = jnp.where((rows >= seg_lo) & (rows < seg_hi), xh, 0.0)
    out = None
    for j in range(w.shape[0]):
        start = HALO + j - taps_left
        term = xh[start:start + c] * w[j:j + 1]
        out = term if out is None else out + term
    return out


def _gdn_kernel(qk_ref, v_ref, g_ref, misc_ref, cw_qk_ref, cw_v_ref, par_ref, nw_ref,
                a3_ref, lv_ref, incl_ref, o_ref,
                qkn_ref, va_ref, acc_ref, lhs1_ref, lhs2_ref, u0_ref, el_ref,
                *, ctx_len, seq):
    c = GLA_CHUNK
    cq = 4 * c
    n_all, n_ctx = seq // c, ctx_len // c
    h = pl.program_id(1)
    lane = lax.broadcasted_iota(jnp.int32, (c, LANES), 1)
    is_q = lane < GDN_DK
    eye = (lax.broadcasted_iota(jnp.int32, (cq, cq), 0)
           == lax.broadcasted_iota(jnp.int32, (cq, cq), 1)).astype(F32)

    def prep(ci, carry):
        r0 = ci * c
        rows = pl.ds(pl.multiple_of(r0, c), c)
        qk = _silu(_conv_rows(qk_ref, r0, c, seq, ctx_len, cw_qk_ref[...], GDN_CONV // 2))
        va = _silu(_conv_rows(v_ref, r0, c, seq, ctx_len, cw_v_ref[...], GDN_CONV // 2))
        sq = qk * qk
        s_q = jnp.sum(jnp.where(is_q, sq, 0.0), axis=-1, keepdims=True)
        s_k = jnp.sum(jnp.where(is_q, 0.0, sq), axis=-1, keepdims=True)
        inv = lax.rsqrt(jnp.where(is_q, s_q, s_k) + NORM_EPS)
        qkn_ref[rows, :] = qk * inv * jnp.where(is_q, GDN_DK ** -0.5, 1.0)
        va_ref[rows, :] = va
        return carry

    lax.fori_loop(0, n_all, prep, 0)

    def stack4(x_f, x_b):
        return jnp.concatenate([x_f[:c], x_b[:c], x_f[c:], x_b[c:]], axis=0)

    def local(gi, carry):
        rows2 = pl.ds(pl.multiple_of(gi * 2 * c, 2 * c), 2 * c)
        qkn = qkn_ref[rows2, :]
        va = va_ref[rows2, :]
        misc = misc_ref[0, rows2, :].astype(F32)
        lane2 = lax.broadcasted_iota(jnp.int32, (2 * c, LANES), 1)

        def pick(idx):
            return jnp.sum(jnp.where(lane2 == idx, misc, 0.0), axis=-1, keepdims=True)

        loga, beta = [], []
        for d in range(2):
            a_log = par_ref[0, 2 * d:2 * d + 1, :]
            dt_b = par_ref[0, 2 * d + 1:2 * d + 2, :]
            loga.append(-jnp.exp(a_log) * jax.nn.softplus(pick(d * GDN_HEADS + h) + dt_b))
            beta.append(jax.nn.sigmoid(pick(2 * GDN_HEADS + d * GDN_HEADS + h)))
        loga = stack4(loga[0], loga[1])
        beta = stack4(beta[0], beta[1])
        qk4 = stack4(qkn, qkn)
        q4 = qk4[:, :GDN_DK]
        k4 = qk4[:, GDN_DK:]
        v4 = stack4(va, va)

        r = _dot(a3_ref[...], jnp.concatenate(_split3(loga), axis=0))
        cum, total = r[:cq], r[cq:]
        cum_t = cum.T
        rel = jnp.concatenate([cum, cum], axis=1) - jnp.concatenate([cum_t, cum_t], axis=0)
        dec = jnp.exp(jnp.minimum(rel, 0.0))
        kb = k4.astype(BF16)
        kq = _dot_nt(jnp.concatenate([kb, q4.astype(BF16)], axis=0), kb)
        base = (beta * dec) * kq[:cq]
        x = eye
        for l in range(lv_ref.shape[0]):
            nm = (base * lv_ref[l]).astype(BF16)
            xb = x.astype(BF16)
            x = x - _dot(_dot(xb, nm).astype(BF16), xb)
        ecum = jnp.exp(cum[:, :GDN_DK])
        rhs = jnp.concatenate([beta * ecum * k4, beta * v4], axis=1).astype(BF16)
        wu = _dot(x.astype(BF16), rhs)
        qkd = kq[cq:] * (incl_ref[...] * dec)
        qt = q4 * ecum
        ktt = (k4 * jnp.exp((total - cum)[:, :GDN_DK])).T
        etot = jnp.exp(total)

        def padded(blk, d):
            z = jnp.zeros_like(blk)
            return jnp.concatenate([blk, z] if d == 0 else [z, blk], axis=1).astype(BF16)

        for j in range(2):
            ci = 2 * gi + j
            step_of = (ci, jnp.where(ci < n_ctx, n_ctx - 1 - ci, n_all + n_ctx - 1 - ci))
            for d in range(2):
                lo = (2 * j + d) * c
                st = step_of[d]
                lhs1_ref[st, d * c:(d + 1) * c, :] = padded(wu[lo:lo + c, :GDN_DK], d)
                lhs1_ref[st, (2 + d) * c:(3 + d) * c, :] = padded(qt[lo:lo + c], d)
                lhs2_ref[st, d * c:(d + 1) * c, :] = padded(qkd[lo:lo + c, lo:lo + c], d)
                lhs2_ref[st, (2 + d) * c:(3 + d) * c, :] = padded(ktt[:, lo:lo + c], d)
                u0_ref[st, d * c:(d + 1) * c, :] = wu[lo:lo + c, GDN_DK:]
                el_ref[d, st] = etot[lo:lo + 8]
        return carry

    lax.fori_loop(0, n_all // 2, local, 0)

    is_fwd_row = lax.broadcasted_iota(jnp.int32, (2 * GDN_DK, LANES), 0) < GDN_DK

    def step(it, s2):
        cf, cb = _chunk_order(it, n_ctx, n_all)
        r1 = _dot(lhs1_ref[it], s2.astype(BF16))
        u = u0_ref[it] - r1[:2 * c]
        r2 = _dot(lhs2_ref[it], u.astype(BF16))
        o2 = r1[2 * c:] + r2[:2 * c]
        acc_ref[0, pl.ds(pl.multiple_of(cf * c, c), c), :] = o2[:c]
        acc_ref[1, pl.ds(pl.multiple_of(cb * c, c), c), :] = o2[c:]
        el2 = jnp.where(is_fwd_row, el_ref[0, it][0:1], el_ref[1, it][0:1])
        return el2 * s2 + r2[2 * c:]

    lax.fori_loop(0, n_all, step, jnp.zeros((2 * GDN_DK, GDN_DV), F32))

    def readout(ci, carry):
        rows = pl.ds(pl.multiple_of(ci * c, c), c)
        o = acc_ref[0, rows, :] + acc_ref[1, rows, :]
        o = o * lax.rsqrt(jnp.mean(o * o, axis=-1, keepdims=True) + NORM_EPS)
        o_ref[0, rows, :] = (o * nw_ref[...] * _silu(g_ref[0, rows, :].astype(F32))).astype(o_ref.dtype)
        return carry

    lax.fori_loop(0, n_all, readout, 0)


def _gdn(z, conv_w, par, norm_w, ctx_len):
    b, s, _ = z.shape
    c = GLA_CHUNK
    n = s // c
    assert n % 2 == 0
    a3, lv, incl = _gdn_consts(c)
    a3 = jnp.asarray(a3, BF16)
    lv = jnp.asarray(lv, F32)
    incl = jnp.asarray(incl, F32)
    kern = functools.partial(_gdn_kernel, ctx_len=ctx_len, seq=s)

    def col(blk):
        return pl.BlockSpec((1, s, LANES), lambda i, h: (i, 0, blk + h))

    def full(a):
        nd = a.ndim
        return pl.BlockSpec(a.shape, lambda i, h: (0,) * nd)

    return pl.pallas_call(
        kern,
        grid=(b, GDN_HEADS),
        in_specs=[col(BLK_GDN_QK), col(BLK_GDN_V), col(BLK_GDN_G),
                  pl.BlockSpec((1, s, LANES), lambda i, h: (i, 0, BLK_GDN_MISC)),
                  pl.BlockSpec((GDN_CONV, LANES), lambda i, h: (0, h)),
                  pl.BlockSpec((GDN_CONV, LANES), lambda i, h: (0, GDN_HEADS + h)),
                  pl.BlockSpec((1, 8, LANES), lambda i, h: (h, 0, 0)),
                  pl.BlockSpec((1, LANES), lambda i, h: (0, 0)),
                  full(a3), full(lv), full(incl)],
        out_specs=pl.BlockSpec((1, s, LANES), lambda i, h: (i, 0, h)),
        out_shape=jax.ShapeDtypeStruct((b, s, GDN_HEADS * GDN_DV), BF16),
        scratch_shapes=[pltpu.VMEM((s, LANES), F32),
                        pltpu.VMEM((s, LANES), F32),
                        pltpu.VMEM((2, s, LANES), F32),
                        pltpu.VMEM((n, 4 * c, 2 * GDN_DK), BF16),
                        pltpu.VMEM((n, 4 * c, 2 * c), BF16),
                        pltpu.VMEM((n, 2 * c, GDN_DV), F32),
                        pltpu.VMEM((2, n, 8, LANES), F32)],
        compiler_params=pltpu.CompilerParams(
            dimension_semantics=("parallel", "parallel"), vmem_limit_bytes=VMEM_LIMIT),
        name="gated_deltanet",
    )(z, z, z, z, conv_w, conv_w, par, norm_w, a3, lv, incl)


def _lru_kernel(x_ref, gate_ref, cw_ref, cb_ref, wd_ref, bd_ref, lam_ref, o_ref,
                af_ref, uf_ref, ab_ref, ub_ref, hf_ref, hb_ref, *, ctx_len, seq, c):
    w = LRU_WIDTH
    n_all, n_ctx = seq // c, ctx_len // c

    def gates(ci, d, a_ref, u_ref):
        r0 = ci * c
        xc = _conv_rows(x_ref, r0, c, seq, ctx_len, cw_ref[...], LRU_CONV // 2) + cb_ref[...]
        y = _dot(xc.astype(BF16), wd_ref[d]) + bd_ref[d]
        r = jax.nn.sigmoid(y[:, :w])
        i = jax.nn.sigmoid(y[:, w:])
        log_a = -LRU_C * r * jax.nn.softplus(-lam_ref[d])
        a = jnp.exp(log_a)
        a_ref[...] = a
        u_ref[...] = jnp.sqrt(1.0 - a * a) * (i * xc)

    def step(it, carry):
        h_f, h_b = carry
        cf, cb = _chunk_order(it, n_ctx, n_all)
        gates(cf, 0, af_ref, uf_ref)
        gates(cb, 1, ab_ref, ub_ref)
        base_f = cf * c
        base_b = cb * c

        def row(t, hh):
            hf, hb = hh
            tb = c - 1 - t
            hf = af_ref[pl.ds(t, 1), :] * hf + uf_ref[pl.ds(t, 1), :]
            hb = ab_ref[pl.ds(tb, 1), :] * hb + ub_ref[pl.ds(tb, 1), :]
            hf_ref[pl.ds(base_f + t, 1), :] = hf
            hb_ref[pl.ds(base_b + tb, 1), :] = hb
            return hf, hb

        return lax.fori_loop(0, c, row, (h_f, h_b), unroll=8)

    zero = jnp.zeros((1, w), F32)
    lax.fori_loop(0, n_all, step, (zero, zero))

    def readout(ci, carry):
        rows = pl.ds(pl.multiple_of(ci * c, c), c)
        hsum = hf_ref[rows, :] + hb_ref[rows, :]
        o_ref[0, rows, :] = (jax.nn.gelu(gate_ref[0, rows, :].astype(F32)) * hsum).astype(o_ref.dtype)
        return carry

    lax.fori_loop(0, n_all, readout, 0)


def _rglru(z, conv_w, conv_b, w_dense, b_dense, lam, ctx_len):
    b, s, _ = z.shape
    w = LRU_WIDTH
    c = _pick_tile(math.gcd(ctx_len, s), (LRU_ROWS, 128, 64))
    kern = functools.partial(_lru_kernel, ctx_len=ctx_len, seq=s, c=c)

    def full(a):
        nd = a.ndim
        return pl.BlockSpec(a.shape, lambda i: (0,) * nd)

    return pl.pallas_call(
        kern,
        grid=(b,),
        in_specs=[pl.BlockSpec((1, s, w), lambda i: (i, 0, BLK_LRU_X * LANES // w)),
                  pl.BlockSpec((1, s, w), lambda i: (i, 0, BLK_LRU_G * LANES // w)),
                  full(conv_w), full(conv_b), full(w_dense), full(b_dense), full(lam)],
        out_specs=pl.BlockSpec((1, s, w), lambda i: (i, 0, 0)),
        out_shape=jax.ShapeDtypeStruct((b, s, w), BF16),
        scratch_shapes=[pltpu.VMEM((c, w), F32)] * 4 + [pltpu.VMEM((s, w), F32)] * 2,
        compiler_params=pltpu.CompilerParams(
            dimension_semantics=("parallel",), vmem_limit_bytes=VMEM_LIMIT),
        name="rglru",
    )(z, z, conv_w, conv_b, w_dense, b_dense, lam)


def _merge_kernel(x_ref, b0_ref, b1_ref, b2_ref, b3_ref, gate_ref, gl_ref, gc_ref, wb_ref, wo_ref,
                  lng_ref, lnb_ref, o_ref, *, ctx_len, tm, alpha):
    t = pl.program_id(1)
    merged = None
    for j, br in enumerate((b0_ref, b1_ref, b2_ref, b3_ref)):
        gate = jax.nn.sigmoid(gate_ref[0, :, j * D_MODEL:(j + 1) * D_MODEL].astype(F32))
        term = gate * _dot(br[0], wb_ref[j])
        merged = term if merged is None else merged + term
    y = _dot(merged.astype(BF16), wo_ref[...])
    rows = _row_ids((tm, 1), t * tm)
    g1 = jnp.where(rows < ctx_len, gc_ref[...], gl_ref[0])
    o_ref[0] = _ln0(alpha * x_ref[0] + g1 * y) * lng_ref[...] + lnb_ref[...]


def _merge(h, branches, z, g_l, g_c, w_branch, w_out, ln_g, ln_b, ctx_len, alpha):
    b, s, d = h.shape
    tm = _pick_tile(s, (256, 128))
    kern = functools.partial(_merge_kernel, ctx_len=ctx_len, tm=tm, alpha=alpha)
    bw = BRANCH_WIDTH

    def tile(width, blk=0):
        return pl.BlockSpec((1, tm, width), lambda i, t: (i, t, blk))

    def full(a):
        nd = a.ndim
        return pl.BlockSpec(a.shape, lambda i, t: (0,) * nd)

    return pl.pallas_call(
        kern,
        grid=(b, s // tm),
        in_specs=[tile(d), tile(bw), tile(bw), tile(bw), tile(bw), tile(N_BRANCH * d, BLK_MG),
                  pl.BlockSpec((1, 1, d), lambda i, t: (i, 0, 0)), full(g_c),
                  full(w_branch), full(w_out), full(ln_g), full(ln_b)],
        out_specs=tile(d),
        out_shape=jax.ShapeDtypeStruct((b, s, d), F32),
        compiler_params=pltpu.CompilerParams(
            dimension_semantics=("parallel", "parallel"), vmem_limit_bytes=VMEM_LIMIT),
        name="merge",
    )(h, *branches, z, g_l, g_c, w_branch, w_out, ln_g, ln_b)


def _ffn_kernel(x_ref, xp_ref, xn_ref, shl_ref, scl_ref, gl_ref, shc_ref, scc_ref, gc_ref,
                wv_ref, wg_ref, cwv_ref, cwg_ref, cbv_ref, cbg_ref, wd_ref, lng_ref, lnb_ref,
                o_ref, xm_ref, xh_ref, acc_ref, *, ctx_len, seq, tm, alpha):
    t = pl.program_id(1)
    j = pl.program_id(2)
    r0 = t * tm

    @pl.when(j == 0)
    def _():
        rows = _row_ids((tm, 1), r0)
        xm = _modulated(x_ref[0], rows, ctx_len, shl_ref[0], scl_ref[0], shc_ref[...], scc_ref[...])
        xm_ref[...] = xm.astype(BF16)
        hrows = jnp.concatenate([_row_ids((8, 1), r0 - 8), _row_ids((8, 1), r0 + tm)], axis=0)
        xh = jnp.concatenate([xp_ref[0], xn_ref[0]], axis=0)
        xhm = _modulated(xh, hrows, ctx_len, shl_ref[0], scl_ref[0], shc_ref[...], scc_ref[...])
        xh_ref[...] = xhm.astype(BF16)
        acc_ref[...] = jnp.zeros_like(acc_ref)

    rows = _row_ids((tm, 1), r0)
    seg_first = (rows == 0) | (rows == ctx_len)
    seg_last = (rows == ctx_len - 1) | (rows == seq - 1)
    first = rows == r0
    last = rows == r0 + tm - 1

    def conv(w_ref, cw_ref, cb_ref):
        u = _dot(xm_ref[...], w_ref[...])
        uh = _dot(xh_ref[...], w_ref[...])
        up = jnp.where(first, uh[7:8], pltpu.roll(u, 1, 0))
        un = jnp.where(last, uh[8:9], pltpu.roll(u, tm - 1, 0))
        up = jnp.where(seg_first, 0.0, up)
        un = jnp.where(seg_last, 0.0, un)
        cw = cw_ref[...]
        return cw[0:1] * up + cw[1:2] * u + cw[2:3] * un + cb_ref[...]

    val = conv(wv_ref, cwv_ref, cbv_ref)
    gate = conv(wg_ref, cwg_ref, cbg_ref)
    act = (_silu(gate) * val).astype(BF16)
    acc_ref[...] += _dot(act, wd_ref[...])

    @pl.when(j == pl.num_programs(2) - 1)
    def _():
        g2 = jnp.where(rows < ctx_len, gc_ref[...], gl_ref[0])
        o_ref[0] = _ln0(alpha * x_ref[0] + g2 * acc_ref[...]) * lng_ref[...] + lnb_ref[...]


def _conv_ffn(h, mods_l, mods_c, w_up, conv_w, conv_b, w_down, ln_g, ln_b, ctx_len, alpha):
    b, s, d = h.shape
    d_ff = w_down.shape[0]
    tm = _pick_tile(s, (768, 384, 256, 128))
    fc = _pick_tile(d_ff, (256, 128))
    nj = d_ff // fc
    kern = functools.partial(_ffn_kernel, ctx_len=ctx_len, seq=s, tm=tm, alpha=alpha)
    nb8 = s // 8

    def vec_l():
        return pl.BlockSpec((1, 1, d), lambda i, t, j: (i, 0, 0))

    def vec_c():
        return pl.BlockSpec((1, d), lambda i, t, j: (0, 0))

    sh_l, sc_l, g_l = mods_l
    sh_c, sc_c, g_c = mods_c
    return pl.pallas_call(
        kern,
        grid=(b, s // tm, nj),
        in_specs=[pl.BlockSpec((1, tm, d), lambda i, t, j: (i, t, 0)),
                  pl.BlockSpec((1, 8, d), lambda i, t, j: (i, jnp.maximum(t * (tm // 8) - 1, 0), 0)),
                  pl.BlockSpec((1, 8, d), lambda i, t, j: (i, jnp.minimum((t + 1) * (tm // 8), nb8 - 1), 0)),
                  vec_l(), vec_l(), vec_l(), vec_c(), vec_c(), vec_c(),
                  pl.BlockSpec((d, fc), lambda i, t, j: (0, j)),
                  pl.BlockSpec((d, fc), lambda i, t, j: (0, nj + j)),
                  pl.BlockSpec((FFN_CONV, fc), lambda i, t, j: (0, j)),
                  pl.BlockSpec((FFN_CONV, fc), lambda i, t, j: (0, nj + j)),
                  pl.BlockSpec((1, fc), lambda i, t, j: (0, j)),
                  pl.BlockSpec((1, fc), lambda i, t, j: (0, nj + j)),
                  pl.BlockSpec((fc, d), lambda i, t, j: (j, 0)),
                  pl.BlockSpec((1, d), lambda i, t, j: (0, 0)),
                  pl.BlockSpec((1, d), lambda i, t, j: (0, 0))],
        out_specs=pl.BlockSpec((1, tm, d), lambda i, t, j: (i, t, 0)),
        out_shape=jax.ShapeDtypeStruct((b, s, d), F32),
        scratch_shapes=[pltpu.VMEM((tm, d), BF16), pltpu.VMEM((16, d), BF16), pltpu.VMEM((tm, d), F32)],
        compiler_params=pltpu.CompilerParams(
            dimension_semantics=("parallel", "parallel", "arbitrary"), vmem_limit_bytes=VMEM_LIMIT),
        name="conv_ffn",
    )(h, h, h, sh_l, sc_l, g_l, sh_c, sc_c, g_c, w_up, w_up, conv_w, conv_w, conv_b, conv_b,
      w_down, ln_g, ln_b)


def _rope_tables(seq, ctx_len):
    quarter = RET_DK // 4
    inv_freq = ROPE_BASE ** (-jnp.arange(quarter, dtype=F32) / quarter)
    p = jnp.arange(seq - ctx_len)
    row = (p // GRID_W).astype(F32)
    col = (p % GRID_W).astype(F32)
    ang_r = row[:, None] * inv_freq[None, :]
    ang_c = col[:, None] * inv_freq[None, :]
    cos = jnp.concatenate([jnp.cos(ang_r)] * 2 + [jnp.cos(ang_c)] * 2, axis=-1)
    sin = jnp.concatenate([-jnp.sin(ang_r), jnp.sin(ang_r), -jnp.sin(ang_c), jnp.sin(ang_c)], axis=-1)
    cos = jnp.concatenate([jnp.ones((ctx_len, RET_DK), F32), cos], axis=0)
    sin = jnp.concatenate([jnp.zeros((ctx_len, RET_DK), F32), sin], axis=0)
    ks = RET_DK ** -0.5
    return jnp.concatenate([cos, cos * ks], axis=-1), jnp.concatenate([sin, sin * ks], axis=-1)


def _block_diag(w):
    nb, k, _ = w.shape
    eye = jnp.eye(nb, dtype=w.dtype)
    return (eye[:, None, :, None] * w[:, :, None, :]).reshape(nb * k, nb * k)


def kernel(x, c, ctx, c_ctx, w_mod, b_mod, w_in, hg_lb, hg_norm, gdn_conv, gdn_a_log, gdn_dt_bias,
           gdn_norm, lru_conv, lru_conv_b, lru_w_a, lru_b_a, lru_w_i, lru_b_i, lru_lam, w_branch,
           w_out, ln_mix_g, ln_mix_b, w_up, ffn_conv, ffn_conv_b, w_down, ln_ffn_g, ln_ffn_b):
    depth = w_in.shape[0]
    batch, seq_l, d = x.shape
    ctx_len = ctx.shape[1]
    seq = ctx_len + seq_l
    alpha = (2.0 * depth) ** 0.25

    perm, n_in = _proj_perm()
    w_in_p = _take_columns(w_in.astype(BF16), perm, n_in)
    gdn_conv_p = _take_columns(gdn_conv, _gdn_conv_perm(), gdn_conv.shape[-1])
    lb = jnp.cumsum(jax.nn.softmax(hg_lb.astype(F32), axis=0), axis=0)
    lb = lb - lb[:1]
    log_gamma = jnp.log1p(-jnp.exp2(-5.0 - jnp.arange(RET_HEADS, dtype=F32)))
    lg_rows = jnp.broadcast_to(log_gamma[:, None, None], (RET_HEADS, 1, LANES))
    cos_t, sin_t = _rope_tables(seq, ctx_len)
    w_branch_b = w_branch.astype(BF16)
    w_out_b = w_out.astype(BF16)
    w_up_b = w_up.astype(BF16)
    w_down_b = w_down.astype(BF16)

    cc = jnp.concatenate([c, c_ctx[None, :], jnp.zeros((7, d), F32)], axis=0)
    mod = _modulation(cc, w_mod, b_mod)

    h = jnp.concatenate([ctx, x], axis=1)
    for i in range(depth):
        ml = mod[i, :batch].reshape(batch, N_MOD, 1, d)
        mc = mod[i, batch].reshape(N_MOD, 1, d)
        sh1, sc1, g1, sh2, sc2, g2 = (ml[:, k] for k in range(N_MOD))
        csh1, csc1, cg1, csh2, csc2, cg2 = (mc[k] for k in range(N_MOD))

        z = _in_projection(h, sh1, sc1, csh1, csc1, w_in_p[i], ctx_len)

        ret = _retention(z, lg_rows, cos_t, sin_t, ctx_len)
        hg = _hgrn2(z, lb[i, 0][None, :], lb[i, 1][None, :], hg_norm[i][None, :], ctx_len)
        par = jnp.stack([gdn_a_log[i, 0], gdn_dt_bias[i, 0], gdn_a_log[i, 1], gdn_dt_bias[i, 1]], axis=1)
        par = jnp.broadcast_to(jnp.pad(par, ((0, 0), (0, 4)))[:, :, None], (GDN_HEADS, 8, LANES))
        gd = _gdn(z, gdn_conv_p[i], par, gdn_norm[i][None, :], ctx_len)
        w_dense = jnp.stack([
            jnp.concatenate([_block_diag(lru_w_a[i, dd]), _block_diag(lru_w_i[i, dd])], axis=1)
            for dd in range(2)]).astype(BF16)
        b_dense = jnp.stack([jnp.concatenate([lru_b_a[i, dd], lru_b_i[i, dd]])[None, :] for dd in range(2)])
        lr = _rglru(z, lru_conv[i], lru_conv_b[i][None, :], w_dense, b_dense,
                    lru_lam[i].astype(F32)[:, None, :], ctx_len)

        h = _merge(h, (ret, hg, gd, lr), z, g1, cg1, w_branch_b[i], w_out_b[i],
                   ln_mix_g[i][None, :], ln_mix_b[i][None, :], ctx_len, alpha)
        h = _conv_ffn(h, (sh2, sc2, g2), (csh2, csc2, cg2), w_up_b[i], ffn_conv[i],
                      ffn_conv_b[i][None, :], w_down_b[i], ln_ffn_g[i][None, :], ln_ffn_b[i][None, :],
                      ctx_len, alpha)
    return h[:, ctx_len:]
```

```python
import functools
import math

import numpy as np
import jax
import jax.numpy as jnp
from jax import lax
from jax.experimental import pallas as pl
from jax.experimental.pallas import tpu as pltpu

F32 = jnp.float32
BF16 = jnp.bfloat16

D_MODEL = 1024
GRID_W = 64
NORM_EPS = 1e-6
ROPE_BASE = 10000.0
RET_HEADS, RET_DK, RET_DV = 4, 64, 128
HG_HEADS, HG_DK, HG_DV = 4, 128, 128
GDN_HEADS, GDN_DK, GDN_DV, GDN_CONV = 4, 64, 128, 4
LRU_WIDTH, LRU_BLOCKS, LRU_CONV, LRU_C = 512, 8, 4, 8.0
N_BRANCH, BRANCH_WIDTH = 4, 512
D_FF, FFN_CONV = 2816, 3
N_MOD = 6

LANES = 128
HALO = 16
VMEM_LIMIT = 56 * 1024 * 1024

BLK_MG = 0
BLK_LRU_X = 32
BLK_LRU_G = 36
BLK_HG_Q = 40
BLK_HG_FF = 44
BLK_HG_FB = 48
BLK_HG_I = 52
BLK_HG_G = 56
BLK_RET_QK = 60
BLK_RET_V = 64
BLK_RET_G = 68
BLK_GDN_QK = 72
BLK_GDN_V = 76
BLK_GDN_G = 80
BLK_GDN_MISC = 84
N_BLK = 85
N_PROJ = N_BLK * LANES

RET_CHUNK = 256
GLA_CHUNK = 64
LRU_ROWS = 256
GDN_GROUP = 2


def _dot(a, b):
    return jnp.dot(a, b, preferred_element_type=F32)


def _dot_nt(a, b):
    return lax.dot_general(a, b, (((1,), (1,)), ((), ())), preferred_element_type=F32)


def _dot_tn(a, b):
    return lax.dot_general(a, b, (((0,), (0,)), ((), ())), preferred_element_type=F32)


def _silu(x):
    return x * jax.nn.sigmoid(x)


def _split3(x):
    hi = x.astype(BF16)
    r1 = x - hi.astype(F32)
    mid = r1.astype(BF16)
    lo = (r1 - mid.astype(F32)).astype(BF16)
    return hi, mid, lo


def _ln0(x):
    mu = jnp.mean(x, axis=-1, keepdims=True)
    xc = x - mu
    var = jnp.mean(xc * xc, axis=-1, keepdims=True)
    return xc * lax.rsqrt(var + NORM_EPS)


def _row_ids(shape, start):
    return start + lax.broadcasted_iota(jnp.int32, shape, 0)


def _chunk_order(it, n_ctx, n_all):
    cb = jnp.where(it < n_ctx, n_ctx - 1 - it, n_all + n_ctx - 1 - it)
    return it, cb


def _level_structs(c):
    t = np.arange(c)
    j = t[None, :]
    segs, masks = [], []
    half = c // 2
    while half >= 1:
        blk = t // (2 * half)
        ref = blk * 2 * half + half - 1
        upper = (t - blk * 2 * half) >= half
        seg_u = (j > ref[:, None]) & (j <= t[:, None])
        seg_l = (j > t[:, None]) & (j <= ref[:, None])
        segs.append(np.where(upper[:, None], seg_u, seg_l))
        masks.append((blk[:, None] == blk[None, :]) & upper[:, None] & (~upper)[None, :])
        half //= 2
    return segs, masks


def _cum_rows(c):
    t = np.arange(c)
    j = t[None, :]
    incl = j <= t[:, None]
    rest = j > t[:, None]
    ones = np.ones((8, c), bool)
    return [incl, rest, ones]


def _flip2(m):
    return m[::-1, ::-1]


@functools.lru_cache(maxsize=None)
def _hg_consts(c):
    segs, masks = _level_structs(c)
    rows_f = segs + _cum_rows(c)
    rows_b = [_flip2(m) for m in segs] + [_flip2(m) for m in _cum_rows(c)[:2]] + _cum_rows(c)[2:]
    masks = masks + [np.eye(c, dtype=bool)]
    a_f = np.concatenate(rows_f, axis=0).astype(np.float32)
    a_b = np.concatenate(rows_b, axis=0).astype(np.float32)
    a3 = np.stack([np.tile(a_f, (1, 3)), np.tile(a_b, (1, 3))])

    def bd4(m_f):
        out = np.zeros((4 * c, 4 * c), np.float32)
        for b in range(4):
            out[b * c:(b + 1) * c, b * c:(b + 1) * c] = m_f if b % 2 == 0 else _flip2(m_f)
        return out

    return a3, np.stack([bd4(m) for m in masks])


@functools.lru_cache(maxsize=None)
def _gdn_consts(c):
    _, masks = _level_structs(c)
    masks = masks[::-1]
    t = np.arange(c)
    incl = t[None, :] <= t[:, None]

    def bd4(m_f):
        out = np.zeros((4 * c, 4 * c), np.float32)
        for b in range(4):
            out[b * c:(b + 1) * c, b * c:(b + 1) * c] = m_f if b % 2 == 0 else _flip2(m_f)
        return out

    ones = np.ones((c, c), bool)
    a = np.concatenate([bd4(incl), bd4(ones)], axis=0)
    a3 = np.tile(a, (1, 3))
    lv = np.stack([bd4(m) for m in masks])
    return a3, lv, bd4(incl)


def _proj_perm():
    names = (('ret_q', 256), ('ret_k', 256), ('ret_v', 512), ('ret_g', 512),
             ('hg_q', 512), ('hg_f_fwd', 512), ('hg_f_bwd', 512), ('hg_i', 512), ('hg_g', 512),
             ('gdn_qkv', 1024), ('gdn_a', 8), ('gdn_b', 8), ('gdn_g', 512),
             ('lru_x', 512), ('lru_gate', 512), ('merge_gate', 4096))
    off, o = {}, 0
    for name, w in names:
        off[name] = o
        o += w
    n_in = o
    perm = np.full((N_PROJ,), n_in, np.int64)

    def put(blk, src, width):
        perm[blk * LANES: blk * LANES + width] = np.arange(src, src + width)

    put(BLK_MG, off['merge_gate'], 4096)
    put(BLK_LRU_X, off['lru_x'], 512)
    put(BLK_LRU_G, off['lru_gate'], 512)
    put(BLK_HG_Q, off['hg_q'], 512)
    put(BLK_HG_FF, off['hg_f_fwd'], 512)
    put(BLK_HG_FB, off['hg_f_bwd'], 512)
    put(BLK_HG_I, off['hg_i'], 512)
    put(BLK_HG_G, off['hg_g'], 512)
    for h in range(RET_HEADS):
        perm[(BLK_RET_QK + h) * LANES: (BLK_RET_QK + h) * LANES + 64] = off['ret_q'] + 64 * h + np.arange(64)
        perm[(BLK_RET_QK + h) * LANES + 64: (BLK_RET_QK + h + 1) * LANES] = off['ret_k'] + 64 * h + np.arange(64)
    put(BLK_RET_V, off['ret_v'], 512)
    put(BLK_RET_G, off['ret_g'], 512)
    gq = off['gdn_qkv']
    for h in range(GDN_HEADS):
        perm[(BLK_GDN_QK + h) * LANES: (BLK_GDN_QK + h) * LANES + 64] = gq + 64 * h + np.arange(64)
        perm[(BLK_GDN_QK + h) * LANES + 64: (BLK_GDN_QK + h + 1) * LANES] = gq + 256 + 64 * h + np.arange(64)
    put(BLK_GDN_V, gq + 512, 512)
    put(BLK_GDN_G, off['gdn_g'], 512)
    put(BLK_GDN_MISC, off['gdn_a'], 8)
    perm[BLK_GDN_MISC * LANES + 8: BLK_GDN_MISC * LANES + 16] = off['gdn_b'] + np.arange(8)
    return perm, n_in


def _take_columns(w, perm, n_src):
    pieces, start = [], 0
    for i in range(1, len(perm) + 1):
        pad = perm[start] == n_src
        if i < len(perm) and ((pad and perm[i] == n_src) or
                              (not pad and perm[i] != n_src and perm[i] == perm[i - 1] + 1)):
            continue
        if pad:
            pieces.append(jnp.zeros(w.shape[:-1] + (i - start,), w.dtype))
        else:
            pieces.append(w[..., int(perm[start]):int(perm[start]) + i - start])
        start = i
    return jnp.concatenate(pieces, axis=-1)


def _gdn_conv_perm():
    p = np.zeros((1024,), np.int64)
    for h in range(GDN_HEADS):
        p[128 * h: 128 * h + 64] = 64 * h + np.arange(64)
        p[128 * h + 64: 128 * h + 128] = 256 + 64 * h + np.arange(64)
    p[512:] = 512 + np.arange(512)
    return p


def _mod_kernel(c_ref, w_ref, b_ref, o_ref):
    s = _silu(c_ref[...])
    o_ref[0] = jnp.dot(s, w_ref[0], preferred_element_type=F32,
                       precision=lax.Precision.HIGHEST) + b_ref[0]


def _modulation(cc, w_mod, b_mod):
    depth, d, n = w_mod.shape
    rows = cc.shape[0]
    tn = 1024
    return pl.pallas_call(
        _mod_kernel,
        grid=(depth, n // tn),
        in_specs=[pl.BlockSpec((rows, d), lambda l, j: (0, 0)),
                  pl.BlockSpec((1, d, tn), lambda l, j: (l, 0, j)),
                  pl.BlockSpec((1, 1, tn), lambda l, j: (l, 0, j))],
        out_specs=pl.BlockSpec((1, rows, tn), lambda l, j: (l, 0, j)),
        out_shape=jax.ShapeDtypeStruct((depth, rows, n), F32),
        compiler_params=pltpu.CompilerParams(vmem_limit_bytes=VMEM_LIMIT),
        name="modulation",
    )(cc, w_mod, b_mod.reshape(depth, 1, n))


def _modulated(x, rows, ctx_len, sh_l, sc_l, sh_c, sc_c):
    is_ctx = rows < ctx_len
    scale = jnp.where(is_ctx, sc_c, sc_l)
    shift = jnp.where(is_ctx, sh_c, sh_l)
    return _ln0(x) * (1.0 + scale) + shift


def _inproj_kernel(x_ref, shl_ref, scl_ref, shc_ref, scc_ref, w_ref, o_ref, xm_ref, *, ctx_len, tm):
    t = pl.program_id(1)

    @pl.when(pl.program_id(2) == 0)
    def _():
        rows = _row_ids((tm, 1), t * tm)
        xm = _modulated(x_ref[0], rows, ctx_len, shl_ref[0], scl_ref[0], shc_ref[...], scc_ref[...])
        xm_ref[...] = xm.astype(BF16)

    o_ref[0] = _dot(xm_ref[...], w_ref[...]).astype(o_ref.dtype)


def _pick_tile(n, cands):
    for c in cands:
        if n % c == 0:
            return c
    return n


def _in_projection(h, sh_l, sc_l, sh_c, sc_c, w_p, ctx_len):
    b, s, d = h.shape
    n = w_p.shape[1]
    tm = _pick_tile(s, (1152, 768, 512, 384, 256, 128))
    tn = _pick_tile(n, (2176, 1280, 640, 128))
    kern = functools.partial(_inproj_kernel, ctx_len=ctx_len, tm=tm)
    return pl.pallas_call(
        kern,
        grid=(b, s // tm, n // tn),
        in_specs=[pl.BlockSpec((1, tm, d), lambda i, t, j: (i, t, 0)),
                  pl.BlockSpec((1, 1, d), lambda i, t, j: (i, 0, 0)),
                  pl.BlockSpec((1, 1, d), lambda i, t, j: (i, 0, 0)),
                  pl.BlockSpec((1, d), lambda i, t, j: (0, 0)),
                  pl.BlockSpec((1, d), lambda i, t, j: (0, 0)),
                  pl.BlockSpec((d, tn), lambda i, t, j: (0, j))],
        out_specs=pl.BlockSpec((1, tm, tn), lambda i, t, j: (i, t, j)),
        out_shape=jax.ShapeDtypeStruct((b, s, n), BF16),
        scratch_shapes=[pltpu.VMEM((tm, d), BF16)],
        compiler_params=pltpu.CompilerParams(
            dimension_semantics=("parallel", "parallel", "arbitrary"),
            vmem_limit_bytes=VMEM_LIMIT),
        name="in_projection",
    )(h, sh_l, sc_l, sh_c, sc_c, w_p)


def _ret_kernel(lg_ref, qk_ref, v_ref, g_ref, cos_ref, sin_ref, o_ref,
                p_ref, q2_ref, ds_ref, sp_ref, *, ctx_len, seq, c):
    n_all, n_ctx = seq // c, ctx_len // c
    dk = RET_DK
    lane = lax.broadcasted_iota(jnp.int32, (c, LANES), 1)
    lg = lg_ref[0]
    pos = lax.broadcasted_iota(jnp.int32, (c, LANES), 0).astype(F32)
    is_q = lane < dk
    fac_f = jnp.exp(jnp.where(is_q, pos + 1.0, c - 1.0 - pos) * lg)
    fac_b = jnp.exp(jnp.where(is_q, c - pos, pos) * lg)
    rr = lax.broadcasted_iota(jnp.int32, (c, c), 0)
    cc = lax.broadcasted_iota(jnp.int32, (c, c), 1)
    dist = jnp.abs(rr - cc).astype(F32)
    dmat = jnp.exp(dist * lg[:, :1]) * jnp.where(rr == cc, 2.0, 1.0)
    gc = jnp.exp(lg * float(c))

    def local(ci, carry):
        rows = pl.ds(pl.multiple_of(ci * c, c), c)
        x = qk_ref[0, rows, :].astype(F32)
        swapped = jnp.where(lane % 32 < 16, pltpu.roll(x, LANES - 16, 1), pltpu.roll(x, 16, 1))
        qk = x * cos_ref[rows, :] + swapped * sin_ref[rows, :]
        p_ref[ci] = (_dot_nt(qk[:, :dk].astype(BF16), qk[:, dk:].astype(BF16)) * dmat).astype(BF16)
        qk_f = qk * fac_f
        qk_b = qk * fac_b
        q2_ref[rows, :] = jnp.where(is_q, qk_f, pltpu.roll(qk_b, dk, 1)).astype(BF16)
        k2 = jnp.where(is_q, pltpu.roll(qk_f, dk, 1), qk_b).astype(BF16)
        ds_ref[ci] = _dot_tn(k2, v_ref[0, rows, :])
        return carry

    lax.fori_loop(0, n_all, local, 0)

    def step(it, carry):
        s_f, s_b = carry
        cf, cb = _chunk_order(it, n_ctx, n_all)
        sp_ref[cf, 0:dk, :] = s_f.astype(BF16)
        sp_ref[cb, dk:2 * dk, :] = s_b.astype(BF16)
        return gc * s_f + ds_ref[cf, 0:dk, :], gc * s_b + ds_ref[cb, dk:2 * dk, :]

    zero = jnp.zeros((dk, RET_DV), F32)
    lax.fori_loop(0, n_all, step, (zero, zero))

    def readout(ci, carry):
        rows = pl.ds(pl.multiple_of(ci * c, c), c)
        lhs = jnp.concatenate([p_ref[ci], q2_ref[rows, :]], axis=1)
        rhs = jnp.concatenate([v_ref[0, rows, :], sp_ref[ci]], axis=0)
        o = _ln0(_dot(lhs, rhs))
        o_ref[0, rows, :] = (o * _silu(g_ref[0, rows, :].astype(F32))).astype(o_ref.dtype)
        return carry

    lax.fori_loop(0, n_all, readout, 0)


def _retention(z, lg_rows, cos_t, sin_t, ctx_len):
    b, s, _ = z.shape
    c = _pick_tile(math.gcd(ctx_len, s), (RET_CHUNK, 128))
    n = s // c
    kern = functools.partial(_ret_kernel, ctx_len=ctx_len, seq=s, c=c)

    def col(blk):
        return pl.BlockSpec((1, s, LANES), lambda i, h: (i, 0, blk + h))

    return pl.pallas_call(
        kern,
        grid=(b, RET_HEADS),
        in_specs=[pl.BlockSpec((1, 1, LANES), lambda i, h: (h, 0, 0)),
                  col(BLK_RET_QK), col(BLK_RET_V), col(BLK_RET_G),
                  pl.BlockSpec((s, LANES), lambda i, h: (0, 0)),
                  pl.BlockSpec((s, LANES), lambda i, h: (0, 0))],
        out_specs=pl.BlockSpec((1, s, LANES), lambda i, h: (i, 0, h)),
        out_shape=jax.ShapeDtypeStruct((b, s, RET_HEADS * RET_DV), BF16),
        scratch_shapes=[pltpu.VMEM((n, c, c), BF16),
                        pltpu.VMEM((s, LANES), BF16),
                        pltpu.VMEM((n, 2 * RET_DK, RET_DV), F32),
                        pltpu.VMEM((n, 2 * RET_DK, RET_DV), BF16)],
        compiler_params=pltpu.CompilerParams(
            dimension_semantics=("parallel", "parallel"), vmem_limit_bytes=VMEM_LIMIT),
        name="retention",
    )(lg_rows, z, z, z, cos_t, sin_t)


def _hg_kernel(q_ref, ff_ref, fb_ref, i_ref, g_ref, lbf_ref, lbb_ref, nw_ref, a3_ref, m_ref, o_ref,
               acc_ref, q2_ref, ds_ref, sp_ref, dec_ref, *, ctx_len, seq):
    c = GLA_CHUNK
    n_all, n_ctx = seq // c, ctx_len // c
    nl = m_ref.shape[0] - 1
    dk = HG_DK

    def stack4(x_f, x_b):
        return jnp.concatenate([x_f[:c], x_b[:c], x_f[c:], x_b[c:]], axis=0)

    def local(gi, carry):
        rows2 = pl.ds(pl.multiple_of(gi * 2 * c, 2 * c), 2 * c)
        q = _silu(q_ref[0, rows2, :].astype(F32))
        v = i_ref[0, rows2, :]
        kk, r = [], []
        for d, (f_ref, lb_ref) in enumerate(((ff_ref, lbf_ref), (fb_ref, lbb_ref))):
            lb = lb_ref[...]
            f = lb + (1.0 - lb) * jax.nn.sigmoid(f_ref[0, rows2, :].astype(F32))
            g = jnp.log(f)
            kk.append(1.0 - f)
            r.append([_dot(a3_ref[d], jnp.concatenate(_split3(g[j * c:(j + 1) * c]), axis=0))
                      for j in range(2)])

        def rows_of(lo, n=c):
            return [r[d][j][lo:lo + n] for j in range(2) for d in range(2)]

        q4 = stack4(q, q)
        k4 = stack4(kk[0], kk[1])
        scores = m_ref[nl] * _dot_nt(q4.astype(BF16), k4.astype(BF16))
        for l in range(nl):
            e = jnp.exp(jnp.concatenate(rows_of(l * c), axis=0))
            scores = scores + m_ref[l] * _dot_nt((q4 * e).astype(BF16), (k4 * e).astype(BF16))
        v4 = jnp.concatenate([v[:c], v[:c], v[c:], v[c:]], axis=0)
        o4 = _dot(scores.astype(BF16), v4)
        acc_ref[rows2, :] = jnp.concatenate([o4[:c] + o4[c:2 * c], o4[2 * c:3 * c] + o4[3 * c:]], axis=0)

        ecum = [jnp.exp(x) for x in rows_of(nl * c)]
        erest = [jnp.exp(x) for x in rows_of((nl + 1) * c)]
        etot = [jnp.exp(x) for x in rows_of((nl + 2) * c, 8)]
        for j in range(2):
            ci = 2 * gi + j
            qj = q[j * c:(j + 1) * c]
            rows = pl.ds(pl.multiple_of(ci * c, c), c)
            q2_ref[rows, :] = jnp.concatenate([qj * ecum[2 * j], qj * ecum[2 * j + 1]], axis=1).astype(BF16)
            k2 = jnp.concatenate([kk[0][j * c:(j + 1) * c] * erest[2 * j],
                                  kk[1][j * c:(j + 1) * c] * erest[2 * j + 1]], axis=1).astype(BF16)
            ds_ref[ci] = _dot_tn(v[j * c:(j + 1) * c], k2)
            dec_ref[ci] = jnp.concatenate([etot[2 * j], etot[2 * j + 1]], axis=1)
        return carry

    lax.fori_loop(0, n_all // 2, local, 0)

    def step(it, carry):
        st_f, st_b = carry
        cf, cb = _chunk_order(it, n_ctx, n_all)
        sp_ref[cf, :, 0:dk] = st_f.astype(BF16)
        sp_ref[cb, :, dk:2 * dk] = st_b.astype(BF16)
        st_f = st_f * dec_ref[cf][0:1, 0:dk] + ds_ref[cf, :, 0:dk]
        st_b = st_b * dec_ref[cb][0:1, dk:2 * dk] + ds_ref[cb, :, dk:2 * dk]
        return st_f, st_b

    zero = jnp.zeros((HG_DV, dk), F32)
    lax.fori_loop(0, n_all, step, (zero, zero))

    def readout(gi, carry):
        for j in range(2):
            ci = 2 * gi + j
            rows = pl.ds(pl.multiple_of(ci * c, c), c)
            o = acc_ref[rows, :] + _dot_nt(q2_ref[rows, :], sp_ref[ci])
            o = o * lax.rsqrt(jnp.mean(o * o, axis=-1, keepdims=True) + NORM_EPS)
            o_ref[0, rows, :] = (o * nw_ref[...] * _silu(g_ref[0, rows, :].astype(F32))).astype(o_ref.dtype)
        return carry

    lax.fori_loop(0, n_all // 2, readout, 0)


def _hgrn2(z, lb_f, lb_b, norm_w, ctx_len):
    b, s, _ = z.shape
    c = GLA_CHUNK
    n = s // c
    assert n % 2 == 0
    a3, masks = _hg_consts(c)
    a3 = jnp.asarray(a3, BF16)
    masks = jnp.asarray(masks, F32)
    kern = functools.partial(_hg_kernel, ctx_len=ctx_len, seq=s)

    def col(blk):
        return pl.BlockSpec((1, s, LANES), lambda i, h: (i, 0, blk + h))

    def full(a):
        nd = a.ndim
        return pl.BlockSpec(a.shape, lambda i, h: (0,) * nd)

    return pl.pallas_call(
        kern,
        grid=(b, HG_HEADS),
        in_specs=[col(BLK_HG_Q), col(BLK_HG_FF), col(BLK_HG_FB), col(BLK_HG_I), col(BLK_HG_G),
                  pl.BlockSpec((1, LANES), lambda i, h: (0, h)),
                  pl.BlockSpec((1, LANES), lambda i, h: (0, h)),
                  pl.BlockSpec((1, LANES), lambda i, h: (0, 0)),
                  full(a3), full(masks)],
        out_specs=pl.BlockSpec((1, s, LANES), lambda i, h: (i, 0, h)),
        out_shape=jax.ShapeDtypeStruct((b, s, HG_HEADS * HG_DV), BF16),
        scratch_shapes=[pltpu.VMEM((s, HG_DV), F32),
                        pltpu.VMEM((s, 2 * HG_DK), BF16),
                        pltpu.VMEM((n, HG_DV, 2 * HG_DK), F32),
                        pltpu.VMEM((n, HG_DV, 2 * HG_DK), BF16),
                        pltpu.VMEM((n, 8, 2 * HG_DK), F32)],
        compiler_params=pltpu.CompilerParams(
            dimension_semantics=("parallel", "parallel"), vmem_limit_bytes=VMEM_LIMIT),
        name="hgrn2",
    )(z, z, z, z, z, lb_f, lb_b, norm_w, a3, masks)


def _conv_rows(ref, r0, c, seq, ctx_len, w, taps_left):
    lo = pl.multiple_of(jnp.maximum(r0 - HALO, 0), HALO)
    hi = pl.multiple_of(jnp.minimum(r0 + c, seq - HALO), HALO)
    xh = jnp.concatenate([ref[0, pl.ds(lo, HALO), :].astype(F32),
                          ref[0, pl.ds(pl.multiple_of(r0, HALO), c), :].astype(F32),
                          ref[0, pl.ds(hi, HALO), :].astype(F32)], axis=0)
    rows = _row_ids((c + 2 * HALO, 1), r0 - HALO)
    seg_lo = jnp.where(r0 < ctx_len, 0, ctx_len)
    seg_hi = jnp.where(r0 < ctx_len, ctx_len, seq)
    xh = jnp.where((rows >= seg_lo) & (rows < seg_hi), xh, 0.0)
    out = None
    for j in range(w.shape[0]):
        start = HALO + j - taps_left
        term = xh[start:start + c] * w[j:j + 1]
        out = term if out is None else out + term
    return out


def _gdn_kernel(qk_ref, v_ref, g_ref, misc_ref, cw_qk_ref, cw_v_ref, par_ref, nw_ref,
                a3_ref, lv_ref, incl_ref, o_ref,
                qkn_ref, va_ref, acc_ref, lhs1_ref, lhs2_ref, u0_ref, el_ref,
                *, ctx_len, seq):
    c = GLA_CHUNK
    cq = 4 * c
    n_all, n_ctx = seq // c, ctx_len // c
    h = pl.program_id(1)
    lane = lax.broadcasted_iota(jnp.int32, (c, LANES), 1)
    is_q = lane < GDN_DK
    eye = (lax.broadcasted_iota(jnp.int32, (cq, cq), 0)
           == lax.broadcasted_iota(jnp.int32, (cq, cq), 1)).astype(F32)

    def prep(ci, carry):
        r0 = ci * c
        rows = pl.ds(pl.multiple_of(r0, c), c)
        qk = _silu(_conv_rows(qk_ref, r0, c, seq, ctx_len, cw_qk_ref[...], GDN_CONV // 2))
        va = _silu(_conv_rows(v_ref, r0, c, seq, ctx_len, cw_v_ref[...], GDN_CONV // 2))
        sq = qk * qk
        s_q = jnp.sum(jnp.where(is_q, sq, 0.0), axis=-1, keepdims=True)
        s_k = jnp.sum(jnp.where(is_q, 0.0, sq), axis=-1, keepdims=True)
        inv = lax.rsqrt(jnp.where(is_q, s_q, s_k) + NORM_EPS)
        qkn_ref[rows, :] = qk * inv * jnp.where(is_q, GDN_DK ** -0.5, 1.0)
        va_ref[rows, :] = va
        return carry

    lax.fori_loop(0, n_all, prep, 0)

    def stack4(x_f, x_b):
        return jnp.concatenate([x_f[:c], x_b[:c], x_f[c:], x_b[c:]], axis=0)

    def local_quad(gi):
        rows2 = pl.ds(pl.multiple_of(gi * 2 * c, 2 * c), 2 * c)
        qkn = qkn_ref[rows2, :]
        va = va_ref[rows2, :]
        misc = misc_ref[0, rows2, :].astype(F32)
        lane2 = lax.broadcasted_iota(jnp.int32, (2 * c, LANES), 1)

        def pick(idx):
            return jnp.sum(jnp.where(lane2 == idx, misc, 0.0), axis=-1, keepdims=True)

        loga, beta = [], []
        for d in range(2):
            a_log = par_ref[0, 2 * d:2 * d + 1, :]
            dt_b = par_ref[0, 2 * d + 1:2 * d + 2, :]
            loga.append(-jnp.exp(a_log) * jax.nn.softplus(pick(d * GDN_HEADS + h) + dt_b))
            beta.append(jax.nn.sigmoid(pick(2 * GDN_HEADS + d * GDN_HEADS + h)))
        loga = stack4(loga[0], loga[1])
        beta = stack4(beta[0], beta[1])
        qk4 = stack4(qkn, qkn)
        q4 = qk4[:, :GDN_DK]
        k4 = qk4[:, GDN_DK:]
        v4 = stack4(va, va)

        r = _dot(a3_ref[...], jnp.concatenate(_split3(loga), axis=0))
        cum, total = r[:cq], r[cq:]
        cum_t = cum.T
        rel = jnp.concatenate([cum, cum], axis=1) - jnp.concatenate([cum_t, cum_t], axis=0)
        dec = jnp.exp(jnp.minimum(rel, 0.0))
        kb = k4.astype(BF16)
        kq = _dot_nt(jnp.concatenate([kb, q4.astype(BF16)], axis=0), kb)
        base = (beta * dec) * kq[:cq]
        x = eye - base * lv_ref[0]
        for l in range(1, lv_ref.shape[0]):
            nm = (base * lv_ref[l]).astype(BF16)
            xb = x.astype(BF16)
            x = x - _dot(_dot(xb, nm).astype(BF16), xb)
        ecum = jnp.exp(cum[:, :GDN_DK])
        rhs = jnp.concatenate([beta * ecum * k4, beta * v4], axis=1).astype(BF16)
        wu = _dot(x.astype(BF16), rhs)
        qkd = kq[cq:] * (incl_ref[...] * dec)
        qt = q4 * ecum
        ktt = (k4 * jnp.exp((total - cum)[:, :GDN_DK])).T
        etot = jnp.exp(total)

        def padded(blk, d):
            z = jnp.zeros_like(blk)
            return jnp.concatenate([blk, z] if d == 0 else [z, blk], axis=1).astype(BF16)

        for j in range(2):
            ci = 2 * gi + j
            step_of = (ci, jnp.where(ci < n_ctx, n_ctx - 1 - ci, n_all + n_ctx - 1 - ci))
            for d in range(2):
                lo = (2 * j + d) * c
                st = step_of[d]
                lhs1_ref[st, d * c:(d + 1) * c, :] = padded(wu[lo:lo + c, :GDN_DK], d)
                lhs1_ref[st, (2 + d) * c:(3 + d) * c, :] = padded(qt[lo:lo + c], d)
                lhs2_ref[st, d * c:(d + 1) * c, :] = padded(qkd[lo:lo + c, lo:lo + c], d)
                lhs2_ref[st, (2 + d) * c:(3 + d) * c, :] = padded(ktt[:, lo:lo + c], d)
                u0_ref[st, d * c:(d + 1) * c, :] = wu[lo:lo + c, GDN_DK:]
                el_ref[d, st] = etot[lo:lo + 8]

    group = _pick_tile(n_all // 2, (GDN_GROUP, 1))

    def local(gj, carry):
        for k in range(group):
            local_quad(gj * group + k)
        return carry

    lax.fori_loop(0, n_all // (2 * group), local, 0)

    is_fwd_row = lax.broadcasted_iota(jnp.int32, (2 * GDN_DK, LANES), 0) < GDN_DK

    def step(it, s2):
        cf, cb = _chunk_order(it, n_ctx, n_all)
        r1 = _dot(lhs1_ref[it], s2.astype(BF16))
        u = u0_ref[it] - r1[:2 * c]
        r2 = _dot(lhs2_ref[it], u.astype(BF16))
        o2 = r1[2 * c:] + r2[:2 * c]
        acc_ref[0, pl.ds(pl.multiple_of(cf * c, c), c), :] = o2[:c]
        acc_ref[1, pl.ds(pl.multiple_of(cb * c, c), c), :] = o2[c:]
        el2 = jnp.where(is_fwd_row, el_ref[0, it][0:1], el_ref[1, it][0:1])
        return el2 * s2 + r2[2 * c:]

    lax.fori_loop(0, n_all, step, jnp.zeros((2 * GDN_DK, GDN_DV), F32))

    def readout(ci, carry):
        rows = pl.ds(pl.multiple_of(ci * c, c), c)
        o = acc_ref[0, rows, :] + acc_ref[1, rows, :]
        o = o * lax.rsqrt(jnp.mean(o * o, axis=-1, keepdims=True) + NORM_EPS)
        o_ref[0, rows, :] = (o * nw_ref[...] * _silu(g_ref[0, rows, :].astype(F32))).astype(o_ref.dtype)
        return carry

    lax.fori_loop(0, n_all, readout, 0)


def _gdn(z, conv_w, par, norm_w, ctx_len):
    b, s, _ = z.shape
    c = GLA_CHUNK
    n = s // c
    assert n % 2 == 0
    a3, lv, incl = _gdn_consts(c)
    a3 = jnp.asarray(a3, BF16)
    lv = jnp.asarray(lv, F32)
    incl = jnp.asarray(incl, F32)
    kern = functools.partial(_gdn_kernel, ctx_len=ctx_len, seq=s)

    def col(blk):
        return pl.BlockSpec((1, s, LANES), lambda i, h: (i, 0, blk + h))

    def full(a):
        nd = a.ndim
        return pl.BlockSpec(a.shape, lambda i, h: (0,) * nd)

    return pl.pallas_call(
        kern,
        grid=(b, GDN_HEADS),
        in_specs=[col(BLK_GDN_QK), col(BLK_GDN_V), col(BLK_GDN_G),
                  pl.BlockSpec((1, s, LANES), lambda i, h: (i, 0, BLK_GDN_MISC)),
                  pl.BlockSpec((GDN_CONV, LANES), lambda i, h: (0, h)),
                  pl.BlockSpec((GDN_CONV, LANES), lambda i, h: (0, GDN_HEADS + h)),
                  pl.BlockSpec((1, 8, LANES), lambda i, h: (h, 0, 0)),
                  pl.BlockSpec((1, LANES), lambda i, h: (0, 0)),
                  full(a3), full(lv), full(incl)],
        out_specs=pl.BlockSpec((1, s, LANES), lambda i, h: (i, 0, h)),
        out_shape=jax.ShapeDtypeStruct((b, s, GDN_HEADS * GDN_DV), BF16),
        scratch_shapes=[pltpu.VMEM((s, LANES), F32),
                        pltpu.VMEM((s, LANES), F32),
                        pltpu.VMEM((2, s, LANES), F32),
                        pltpu.VMEM((n, 4 * c, 2 * GDN_DK), BF16),
                        pltpu.VMEM((n, 4 * c, 2 * c), BF16),
                        pltpu.VMEM((n, 2 * c, GDN_DV), F32),
                        pltpu.VMEM((2, n, 8, LANES), F32)],
        compiler_params=pltpu.CompilerParams(
            dimension_semantics=("parallel", "parallel"), vmem_limit_bytes=VMEM_LIMIT),
        name="gated_deltanet",
    )(z, z, z, z, conv_w, conv_w, par, norm_w, a3, lv, incl)


def _lru_kernel(x_ref, gate_ref, cw_ref, cb_ref, wd_ref, bd_ref, lam_ref, o_ref,
                af_ref, uf_ref, ab_ref, ub_ref, hf_ref, hb_ref, *, ctx_len, seq, c):
    w = LRU_WIDTH
    n_all, n_ctx = seq // c, ctx_len // c

    def gates(ci, d, a_ref, u_ref):
        r0 = ci * c
        xc = _conv_rows(x_ref, r0, c, seq, ctx_len, cw_ref[...], LRU_CONV // 2) + cb_ref[...]
        y = _dot(xc.astype(BF16), wd_ref[d]) + bd_ref[d]
        r = jax.nn.sigmoid(y[:, :w])
        i = jax.nn.sigmoid(y[:, w:])
        log_a = -LRU_C * r * jax.nn.softplus(-lam_ref[d])
        a = jnp.exp(log_a)
        a_ref[...] = a
        u_ref[...] = jnp.sqrt(1.0 - a * a) * (i * xc)

    def step(it, carry):
        h_f, h_b = carry
        cf, cb = _chunk_order(it, n_ctx, n_all)
        gates(cf, 0, af_ref, uf_ref)
        gates(cb, 1, ab_ref, ub_ref)
        base_f = cf * c
        base_b = cb * c

        def row(t, hh):
            hf, hb = hh
            tb = c - 1 - t
            hf = af_ref[pl.ds(t, 1), :] * hf + uf_ref[pl.ds(t, 1), :]
            hb = ab_ref[pl.ds(tb, 1), :] * hb + ub_ref[pl.ds(tb, 1), :]
            hf_ref[pl.ds(base_f + t, 1), :] = hf
            hb_ref[pl.ds(base_b + tb, 1), :] = hb
            return hf, hb

        return lax.fori_loop(0, c, row, (h_f, h_b), unroll=8)

    zero = jnp.zeros((1, w), F32)
    lax.fori_loop(0, n_all, step, (zero, zero))

    def readout(ci, carry):
        rows = pl.ds(pl.multiple_of(ci * c, c), c)
        hsum = hf_ref[rows, :] + hb_ref[rows, :]
        o_ref[0, rows, :] = (jax.nn.gelu(gate_ref[0, rows, :].astype(F32)) * hsum).astype(o_ref.dtype)
        return carry

    lax.fori_loop(0, n_all, readout, 0)


def _rglru(z, conv_w, conv_b, w_dense, b_dense, lam, ctx_len):
    b, s, _ = z.shape
    w = LRU_WIDTH
    c = _pick_tile(math.gcd(ctx_len, s), (LRU_ROWS, 128, 64))
    kern = functools.partial(_lru_kernel, ctx_len=ctx_len, seq=s, c=c)

    def full(a):
        nd = a.ndim
        return pl.BlockSpec(a.shape, lambda i: (0,) * nd)

    return pl.pallas_call(
        kern,
        grid=(b,),
        in_specs=[pl.BlockSpec((1, s, w), lambda i: (i, 0, BLK_LRU_X * LANES // w)),
                  pl.BlockSpec((1, s, w), lambda i: (i, 0, BLK_LRU_G * LANES // w)),
                  full(conv_w), full(conv_b), full(w_dense), full(b_dense), full(lam)],
        out_specs=pl.BlockSpec((1, s, w), lambda i: (i, 0, 0)),
        out_shape=jax.ShapeDtypeStruct((b, s, w), BF16),
        scratch_shapes=[pltpu.VMEM((c, w), F32)] * 4 + [pltpu.VMEM((s, w), F32)] * 2,
        compiler_params=pltpu.CompilerParams(
            dimension_semantics=("parallel",), vmem_limit_bytes=VMEM_LIMIT),
        name="rglru",
    )(z, z, conv_w, conv_b, w_dense, b_dense, lam)


def _merge_kernel(x_ref, b0_ref, b1_ref, b2_ref, b3_ref, gate_ref, gl_ref, gc_ref, wb_ref, wo_ref,
                  lng_ref, lnb_ref, o_ref, *, ctx_len, tm, alpha):
    t = pl.program_id(1)
    merged = None
    for j, br in enumerate((b0_ref, b1_ref, b2_ref, b3_ref)):
        gate = jax.nn.sigmoid(gate_ref[0, :, j * D_MODEL:(j + 1) * D_MODEL].astype(F32))
        term = gate * _dot(br[0], wb_ref[j])
        merged = term if merged is None else merged + term
    y = _dot(merged.astype(BF16), wo_ref[...])
    rows = _row_ids((tm, 1), t * tm)
    g1 = jnp.where(rows < ctx_len, gc_ref[...], gl_ref[0])
    o_ref[0] = _ln0(alpha * x_ref[0] + g1 * y) * lng_ref[...] + lnb_ref[...]


def _merge(h, branches, z, g_l, g_c, w_branch, w_out, ln_g, ln_b, ctx_len, alpha):
    b, s, d = h.shape
    tm = _pick_tile(s, (256, 128))
    kern = functools.partial(_merge_kernel, ctx_len=ctx_len, tm=tm, alpha=alpha)
    bw = BRANCH_WIDTH

    def tile(width, blk=0):
        return pl.BlockSpec((1, tm, width), lambda i, t: (i, t, blk))

    def full(a):
        nd = a.ndim
        return pl.BlockSpec(a.shape, lambda i, t: (0,) * nd)

    return pl.pallas_call(
        kern,
        grid=(b, s // tm),
        in_specs=[tile(d), tile(bw), tile(bw), tile(bw), tile(bw), tile(N_BRANCH * d, BLK_MG),
                  pl.BlockSpec((1, 1, d), lambda i, t: (i, 0, 0)), full(g_c),
                  full(w_branch), full(w_out), full(ln_g), full(ln_b)],
        out_specs=tile(d),
        out_shape=jax.ShapeDtypeStruct((b, s, d), F32),
        compiler_params=pltpu.CompilerParams(
            dimension_semantics=("parallel", "parallel"), vmem_limit_bytes=VMEM_LIMIT),
        name="merge",
    )(h, *branches, z, g_l, g_c, w_branch, w_out, ln_g, ln_b)


FFN_ROWS = 64
FFN_COLS = 256


def _ffn_kernel(x_ref, xp_ref, xn_ref, shl_ref, scl_ref, gl_ref, shc_ref, scc_ref, gc_ref,
                wu_ref, cw_ref, cb_ref, wd_ref, lng_ref, lnb_ref,
                o_ref, xm_ref, uv_ref, ug_ref, act_ref, *, ctx_len, seq, tm, alpha):
    d_ff = wd_ref.shape[0]
    fc = _pick_tile(d_ff, (FFN_COLS, LANES))
    rb = _pick_tile(tm, (FFN_ROWS, 8))
    r0 = pl.program_id(1) * tm

    x = x_ref[0]
    x_ext = jnp.concatenate([xp_ref[0], x, xn_ref[0]], axis=0)
    xm = _modulated(x_ext, _row_ids((tm + 16, 1), r0 - 8), ctx_len,
                    shl_ref[0], scl_ref[0], shc_ref[...], scc_ref[...])
    xm_ref[...] = xm.astype(BF16)

    rows = _row_ids((tm, 1), r0)
    m_up = jnp.where((rows == 0) | (rows == ctx_len), 0.0, 1.0)
    m_dn = jnp.where((rows == ctx_len - 1) | (rows == seq - 1), 0.0, 1.0)

    def conv(u_ref, a, mu, md, cw, cb):
        up = u_ref[7 + a:7 + a + rb, :]
        mid = u_ref[8 + a:8 + a + rb, :]
        dn = u_ref[9 + a:9 + a + rb, :]
        return cw[0:1] * (mu * up) + cw[1:2] * mid + cw[2:3] * (md * dn) + cb

    for j in range(d_ff // fc):
        cv = slice(j * fc, (j + 1) * fc)
        cg = slice(d_ff + j * fc, d_ff + (j + 1) * fc)
        uv_ref[...] = _dot(xm_ref[...], wu_ref[:, cv])
        ug_ref[...] = _dot(xm_ref[...], wu_ref[:, cg])
        for a in range(0, tm, rb):
            mu, md = m_up[a:a + rb], m_dn[a:a + rb]
            val = conv(uv_ref, a, mu, md, cw_ref[:, cv], cb_ref[:, cv])
            gate = conv(ug_ref, a, mu, md, cw_ref[:, cg], cb_ref[:, cg])
            act_ref[a:a + rb, cv] = (_silu(gate) * val).astype(BF16)

    f = _dot(act_ref[...], wd_ref[...])
    g2 = jnp.where(rows < ctx_len, gc_ref[...], gl_ref[0])
    o_ref[0] = _ln0(alpha * x + g2 * f) * lng_ref[...] + lnb_ref[...]


def _conv_ffn(h, mods_l, mods_c, w_up, conv_w, conv_b, w_down, ln_g, ln_b, ctx_len, alpha):
    b, s, d = h.shape
    d_ff = w_down.shape[0]
    tm = _pick_tile(s, (768, 384, 256, 128))
    fc = _pick_tile(d_ff, (FFN_COLS, LANES))
    kern = functools.partial(_ffn_kernel, ctx_len=ctx_len, seq=s, tm=tm, alpha=alpha)
    nb8 = s // 8

    def vec_l():
        return pl.BlockSpec((1, 1, d), lambda i, t: (i, 0, 0))

    def resident(a):
        nd = a.ndim
        return pl.BlockSpec(a.shape, lambda i, t: (0,) * nd, pipeline_mode=pl.Buffered(1))

    sh_l, sc_l, g_l = mods_l
    sh_c, sc_c, g_c = mods_c
    return pl.pallas_call(
        kern,
        grid=(b, s // tm),
        in_specs=[pl.BlockSpec((1, tm, d), lambda i, t: (i, t, 0)),
                  pl.BlockSpec((1, 8, d), lambda i, t: (i, jnp.maximum(t * (tm // 8) - 1, 0), 0)),
                  pl.BlockSpec((1, 8, d), lambda i, t: (i, jnp.minimum((t + 1) * (tm // 8), nb8 - 1), 0)),
                  vec_l(), vec_l(), vec_l(), resident(sh_c), resident(sc_c), resident(g_c),
                  resident(w_up), resident(conv_w), resident(conv_b), resident(w_down),
                  resident(ln_g), resident(ln_b)],
        out_specs=pl.BlockSpec((1, tm, d), lambda i, t: (i, t, 0)),
        out_shape=jax.ShapeDtypeStruct((b, s, d), F32),
        scratch_shapes=[pltpu.VMEM((tm + 16, d), BF16),
                        pltpu.VMEM((tm + 16, fc), F32), pltpu.VMEM((tm + 16, fc), F32),
                        pltpu.VMEM((tm, d_ff), BF16)],
        compiler_params=pltpu.CompilerParams(
            dimension_semantics=("parallel", "parallel"), vmem_limit_bytes=VMEM_LIMIT),
        name="conv_ffn",
    )(h, h, h, sh_l, sc_l, g_l, sh_c, sc_c, g_c, w_up, conv_w, conv_b, w_down, ln_g, ln_b)


def _rope_tables(seq, ctx_len):
    quarter = RET_DK // 4
    inv_freq = ROPE_BASE ** (-jnp.arange(quarter, dtype=F32) / quarter)
    p = jnp.arange(seq - ctx_len)
    row = (p // GRID_W).astype(F32)
    col = (p % GRID_W).astype(F32)
    ang_r = row[:, None] * inv_freq[None, :]
    ang_c = col[:, None] * inv_freq[None, :]
    cos = jnp.concatenate([jnp.cos(ang_r)] * 2 + [jnp.cos(ang_c)] * 2, axis=-1)
    sin = jnp.concatenate([-jnp.sin(ang_r), jnp.sin(ang_r), -jnp.sin(ang_c), jnp.sin(ang_c)], axis=-1)
    cos = jnp.concatenate([jnp.ones((ctx_len, RET_DK), F32), cos], axis=0)
    sin = jnp.concatenate([jnp.zeros((ctx_len, RET_DK), F32), sin], axis=0)
    ks = RET_DK ** -0.5
    return jnp.concatenate([cos, cos * ks], axis=-1), jnp.concatenate([sin, sin * ks], axis=-1)


def _block_diag(w):
    nb, k, _ = w.shape
    eye = jnp.eye(nb, dtype=w.dtype)
    return (eye[:, None, :, None] * w[:, :, None, :]).reshape(nb * k, nb * k)


def kernel(x, c, ctx, c_ctx, w_mod, b_mod, w_in, hg_lb, hg_norm, gdn_conv, gdn_a_log, gdn_dt_bias,
           gdn_norm, lru_conv, lru_conv_b, lru_w_a, lru_b_a, lru_w_i, lru_b_i, lru_lam, w_branch,
           w_out, ln_mix_g, ln_mix_b, w_up, ffn_conv, ffn_conv_b, w_down, ln_ffn_g, ln_ffn_b):
    depth = w_in.shape[0]
    batch, seq_l, d = x.shape
    ctx_len = ctx.shape[1]
    seq = ctx_len + seq_l
    alpha = (2.0 * depth) ** 0.25

    perm, n_in = _proj_perm()
    w_in_p = _take_columns(w_in.astype(BF16), perm, n_in)
    gdn_conv_p = _take_columns(gdn_conv, _gdn_conv_perm(), gdn_conv.shape[-1])
    lb = jnp.cumsum(jax.nn.softmax(hg_lb.astype(F32), axis=0), axis=0)
    lb = lb - lb[:1]
    log_gamma = jnp.log1p(-jnp.exp2(-5.0 - jnp.arange(RET_HEADS, dtype=F32)))
    lg_rows = jnp.broadcast_to(log_gamma[:, None, None], (RET_HEADS, 1, LANES))
    cos_t, sin_t = _rope_tables(seq, ctx_len)
    w_branch_b = w_branch.astype(BF16)
    w_out_b = w_out.astype(BF16)
    w_up_b = w_up.astype(BF16)
    w_down_b = w_down.astype(BF16)

    cc = jnp.concatenate([c, c_ctx[None, :], jnp.zeros((7, d), F32)], axis=0)
    mod = _modulation(cc, w_mod, b_mod)

    h = jnp.concatenate([ctx, x], axis=1)
    for i in range(depth):
        ml = mod[i, :batch].reshape(batch, N_MOD, 1, d)
        mc = mod[i, batch].reshape(N_MOD, 1, d)
        sh1, sc1, g1, sh2, sc2, g2 = (ml[:, k] for k in range(N_MOD))
        csh1, csc1, cg1, csh2, csc2, cg2 = (mc[k] for k in range(N_MOD))

        z = _in_projection(h, sh1, sc1, csh1, csc1, w_in_p[i], ctx_len)

        ret = _retention(z, lg_rows, cos_t, sin_t, ctx_len)
        hg = _hgrn2(z, lb[i, 0][None, :], lb[i, 1][None, :], hg_norm[i][None, :], ctx_len)
        par = jnp.stack([gdn_a_log[i, 0], gdn_dt_bias[i, 0], gdn_a_log[i, 1], gdn_dt_bias[i, 1]], axis=1)
        par = jnp.broadcast_to(jnp.pad(par, ((0, 0), (0, 4)))[:, :, None], (GDN_HEADS, 8, LANES))
        gd = _gdn(z, gdn_conv_p[i], par, gdn_norm[i][None, :], ctx_len)
        w_dense = jnp.stack([
            jnp.concatenate([_block_diag(lru_w_a[i, dd]), _block_diag(lru_w_i[i, dd])], axis=1)
            for dd in range(2)]).astype(BF16)
        b_dense = jnp.stack([jnp.concatenate([lru_b_a[i, dd], lru_b_i[i, dd]])[None, :] for dd in range(2)])
        lr = _rglru(z, lru_conv[i], lru_conv_b[i][None, :], w_dense, b_dense,
                    lru_lam[i].astype(F32)[:, None, :], ctx_len)

        h = _merge(h, (ret, hg, gd, lr), z, g1, cg1, w_branch_b[i], w_out_b[i],
                   ln_mix_g[i][None, :], ln_mix_b[i][None, :], ctx_len, alpha)
        h = _conv_ffn(h, (sh2, sc2, g2), (csh2, csc2, cg2), w_up_b[i], ffn_conv[i],
                      ffn_conv_b[i][None, :], w_down_b[i], ln_ffn_g[i][None, :], ln_ffn_b[i][None, :],
                      ctx_len, alpha)
    return h[:, ctx_len:]
```

```python
import functools
import math

import numpy as np
import jax
import jax.numpy as jnp
from jax import lax
from jax.experimental import pallas as pl
from jax.experimental.pallas import tpu as pltpu

F32 = jnp.float32
BF16 = jnp.bfloat16

D_MODEL = 1024
GRID_W = 64
NORM_EPS = 1e-6
ROPE_BASE = 10000.0
RET_HEADS, RET_DK, RET_DV = 4, 64, 128
HG_HEADS, HG_DK, HG_DV = 4, 128, 128
GDN_HEADS, GDN_DK, GDN_DV, GDN_CONV = 4, 64, 128, 4
LRU_WIDTH, LRU_BLOCKS, LRU_CONV, LRU_C = 512, 8, 4, 8.0
N_BRANCH, BRANCH_WIDTH = 4, 512
D_FF, FFN_CONV = 2816, 3
N_MOD = 6

LANES = 128
HALO = 16
VMEM_LIMIT = 56 * 1024 * 1024

BLK_MG = 0
BLK_LRU_X = 32
BLK_LRU_G = 36
BLK_HG_Q = 40
BLK_HG_FF = 44
BLK_HG_FB = 48
BLK_HG_I = 52
BLK_HG_G = 56
BLK_RET_QK = 60
BLK_RET_V = 64
BLK_RET_G = 68
BLK_GDN_QK = 72
BLK_GDN_V = 76
BLK_GDN_G = 80
BLK_GDN_MISC = 84
N_BLK = 85
N_PROJ = N_BLK * LANES

RET_CHUNK = 256
GLA_CHUNK = 64
LRU_ROWS = 256
GDN_GROUP = 3


def _dot(a, b):
    return jnp.dot(a, b, preferred_element_type=F32)


def _dot_nt(a, b):
    return lax.dot_general(a, b, (((1,), (1,)), ((), ())), preferred_element_type=F32)


def _dot_tn(a, b):
    return lax.dot_general(a, b, (((0,), (0,)), ((), ())), preferred_element_type=F32)


def _silu(x):
    return x * jax.nn.sigmoid(x)


def _split3(x):
    hi = x.astype(BF16)
    r1 = x - hi.astype(F32)
    mid = r1.astype(BF16)
    lo = (r1 - mid.astype(F32)).astype(BF16)
    return hi, mid, lo


def _ln0(x):
    mu = jnp.mean(x, axis=-1, keepdims=True)
    xc = x - mu
    var = jnp.mean(xc * xc, axis=-1, keepdims=True)
    return xc * lax.rsqrt(var + NORM_EPS)


def _row_ids(shape, start):
    return start + lax.broadcasted_iota(jnp.int32, shape, 0)


def _chunk_order(it, n_ctx, n_all):
    cb = jnp.where(it < n_ctx, n_ctx - 1 - it, n_all + n_ctx - 1 - it)
    return it, cb


def _level_structs(c):
    t = np.arange(c)
    j = t[None, :]
    segs, masks = [], []
    half = c // 2
    while half >= 1:
        blk = t // (2 * half)
        ref = blk * 2 * half + half - 1
        upper = (t - blk * 2 * half) >= half
        seg_u = (j > ref[:, None]) & (j <= t[:, None])
        seg_l = (j > t[:, None]) & (j <= ref[:, None])
        segs.append(np.where(upper[:, None], seg_u, seg_l))
        masks.append((blk[:, None] == blk[None, :]) & upper[:, None] & (~upper)[None, :])
        half //= 2
    return segs, masks


def _cum_rows(c):
    t = np.arange(c)
    j = t[None, :]
    incl = j <= t[:, None]
    rest = j > t[:, None]
    ones = np.ones((8, c), bool)
    return [incl, rest, ones]


def _flip2(m):
    return m[::-1, ::-1]


@functools.lru_cache(maxsize=None)
def _hg_consts(c):
    segs, masks = _level_structs(c)
    rows_f = segs + _cum_rows(c)
    rows_b = [_flip2(m) for m in segs] + [_flip2(m) for m in _cum_rows(c)[:2]] + _cum_rows(c)[2:]
    masks = masks + [np.eye(c, dtype=bool)]
    a_f = np.concatenate(rows_f, axis=0).astype(np.float32)
    a_b = np.concatenate(rows_b, axis=0).astype(np.float32)
    a3 = np.stack([np.tile(a_f, (1, 3)), np.tile(a_b, (1, 3))])

    def bd4(m_f):
        out = np.zeros((4 * c, 4 * c), np.float32)
        for b in range(4):
            out[b * c:(b + 1) * c, b * c:(b + 1) * c] = m_f if b % 2 == 0 else _flip2(m_f)
        return out

    return a3, np.stack([bd4(m) for m in masks])


@functools.lru_cache(maxsize=None)
def _gdn_consts(c):
    _, masks = _level_structs(c)
    masks = masks[::-1]
    t = np.arange(c)
    incl = t[None, :] <= t[:, None]

    def bd4(m_f):
        out = np.zeros((4 * c, 4 * c), np.float32)
        for b in range(4):
            out[b * c:(b + 1) * c, b * c:(b + 1) * c] = m_f if b % 2 == 0 else _flip2(m_f)
        return out

    ones = np.ones((c, c), bool)
    a = np.concatenate([bd4(incl), bd4(ones)], axis=0)
    lv = np.stack([bd4(m) for m in masks])
    return a, lv, bd4(incl)


def _proj_perm():
    names = (('ret_q', 256), ('ret_k', 256), ('ret_v', 512), ('ret_g', 512),
             ('hg_q', 512), ('hg_f_fwd', 512), ('hg_f_bwd', 512), ('hg_i', 512), ('hg_g', 512),
             ('gdn_qkv', 1024), ('gdn_a', 8), ('gdn_b', 8), ('gdn_g', 512),
             ('lru_x', 512), ('lru_gate', 512), ('merge_gate', 4096))
    off, o = {}, 0
    for name, w in names:
        off[name] = o
        o += w
    n_in = o
    perm = np.full((N_PROJ,), n_in, np.int64)

    def put(blk, src, width):
        perm[blk * LANES: blk * LANES + width] = np.arange(src, src + width)

    put(BLK_MG, off['merge_gate'], 4096)
    put(BLK_LRU_X, off['lru_x'], 512)
    put(BLK_LRU_G, off['lru_gate'], 512)
    put(BLK_HG_Q, off['hg_q'], 512)
    put(BLK_HG_FF, off['hg_f_fwd'], 512)
    put(BLK_HG_FB, off['hg_f_bwd'], 512)
    put(BLK_HG_I, off['hg_i'], 512)
    put(BLK_HG_G, off['hg_g'], 512)
    for h in range(RET_HEADS):
        perm[(BLK_RET_QK + h) * LANES: (BLK_RET_QK + h) * LANES + 64] = off['ret_q'] + 64 * h + np.arange(64)
        perm[(BLK_RET_QK + h) * LANES + 64: (BLK_RET_QK + h + 1) * LANES] = off['ret_k'] + 64 * h + np.arange(64)
    put(BLK_RET_V, off['ret_v'], 512)
    put(BLK_RET_G, off['ret_g'], 512)
    gq = off['gdn_qkv']
    for h in range(GDN_HEADS):
        perm[(BLK_GDN_QK + h) * LANES: (BLK_GDN_QK + h) * LANES + 64] = gq + 64 * h + np.arange(64)
        perm[(BLK_GDN_QK + h) * LANES + 64: (BLK_GDN_QK + h + 1) * LANES] = gq + 256 + 64 * h + np.arange(64)
    put(BLK_GDN_V, gq + 512, 512)
    put(BLK_GDN_G, off['gdn_g'], 512)
    put(BLK_GDN_MISC, off['gdn_a'], 8)
    perm[BLK_GDN_MISC * LANES + 8: BLK_GDN_MISC * LANES + 16] = off['gdn_b'] + np.arange(8)
    return perm, n_in


def _take_columns(w, perm, n_src):
    pieces, start = [], 0
    for i in range(1, len(perm) + 1):
        pad = perm[start] == n_src
        if i < len(perm) and ((pad and perm[i] == n_src) or
                              (not pad and perm[i] != n_src and perm[i] == perm[i - 1] + 1)):
            continue
        if pad:
            pieces.append(jnp.zeros(w.shape[:-1] + (i - start,), w.dtype))
        else:
            pieces.append(w[..., int(perm[start]):int(perm[start]) + i - start])
        start = i
    return jnp.concatenate(pieces, axis=-1)


def _gdn_conv_perm():
    p = np.zeros((1024,), np.int64)
    for h in range(GDN_HEADS):
        p[128 * h: 128 * h + 64] = 64 * h + np.arange(64)
        p[128 * h + 64: 128 * h + 128] = 256 + 64 * h + np.arange(64)
    p[512:] = 512 + np.arange(512)
    return p


def _mod_kernel(c_ref, w_ref, b_ref, o_ref):
    s = _silu(c_ref[...])
    o_ref[0] = jnp.dot(s, w_ref[0], preferred_element_type=F32,
                       precision=lax.Precision.HIGHEST) + b_ref[0]


def _modulation(cc, w_mod, b_mod):
    depth, d, n = w_mod.shape
    rows = cc.shape[0]
    tn = 1024
    return pl.pallas_call(
        _mod_kernel,
        grid=(depth, n // tn),
        in_specs=[pl.BlockSpec((rows, d), lambda l, j: (0, 0)),
                  pl.BlockSpec((1, d, tn), lambda l, j: (l, 0, j)),
                  pl.BlockSpec((1, 1, tn), lambda l, j: (l, 0, j))],
        out_specs=pl.BlockSpec((1, rows, tn), lambda l, j: (l, 0, j)),
        out_shape=jax.ShapeDtypeStruct((depth, rows, n), F32),
        compiler_params=pltpu.CompilerParams(vmem_limit_bytes=VMEM_LIMIT),
        name="modulation",
    )(cc, w_mod, b_mod.reshape(depth, 1, n))


def _modulated(x, rows, ctx_len, sh_l, sc_l, sh_c, sc_c):
    is_ctx = rows < ctx_len
    scale = jnp.where(is_ctx, sc_c, sc_l)
    shift = jnp.where(is_ctx, sh_c, sh_l)
    return _ln0(x) * (1.0 + scale) + shift


def _inproj_kernel(x_ref, shl_ref, scl_ref, shc_ref, scc_ref, w_ref, o_ref, xm_ref, *, ctx_len, tm):
    t = pl.program_id(1)

    @pl.when(pl.program_id(2) == 0)
    def _():
        rows = _row_ids((tm, 1), t * tm)
        xm = _modulated(x_ref[0], rows, ctx_len, shl_ref[0], scl_ref[0], shc_ref[...], scc_ref[...])
        xm_ref[...] = xm.astype(BF16)

    o_ref[0] = _dot(xm_ref[...], w_ref[...]).astype(o_ref.dtype)


def _pick_tile(n, cands):
    for c in cands:
        if n % c == 0:
            return c
    return n


def _in_projection(h, sh_l, sc_l, sh_c, sc_c, w_p, ctx_len):
    b, s, d = h.shape
    n = w_p.shape[1]
    tm = _pick_tile(s, (1152, 768, 512, 384, 256, 128))
    tn = _pick_tile(n, (2176, 1280, 640, 128))
    kern = functools.partial(_inproj_kernel, ctx_len=ctx_len, tm=tm)
    return pl.pallas_call(
        kern,
        grid=(b, s // tm, n // tn),
        in_specs=[pl.BlockSpec((1, tm, d), lambda i, t, j: (i, t, 0)),
                  pl.BlockSpec((1, 1, d), lambda i, t, j: (i, 0, 0)),
                  pl.BlockSpec((1, 1, d), lambda i, t, j: (i, 0, 0)),
                  pl.BlockSpec((1, d), lambda i, t, j: (0, 0)),
                  pl.BlockSpec((1, d), lambda i, t, j: (0, 0)),
                  pl.BlockSpec((d, tn), lambda i, t, j: (0, j))],
        out_specs=pl.BlockSpec((1, tm, tn), lambda i, t, j: (i, t, j)),
        out_shape=jax.ShapeDtypeStruct((b, s, n), BF16),
        scratch_shapes=[pltpu.VMEM((tm, d), BF16)],
        compiler_params=pltpu.CompilerParams(
            dimension_semantics=("parallel", "parallel", "arbitrary"),
            vmem_limit_bytes=VMEM_LIMIT),
        name="in_projection",
    )(h, sh_l, sc_l, sh_c, sc_c, w_p)


def _ret_kernel(lg_ref, qk_ref, v_ref, g_ref, cos_ref, sin_ref, o_ref,
                p_ref, q2_ref, ds_ref, sp_ref, *, ctx_len, seq, c):
    n_all, n_ctx = seq // c, ctx_len // c
    dk = RET_DK
    lane = lax.broadcasted_iota(jnp.int32, (c, LANES), 1)
    lg = lg_ref[0]
    pos = lax.broadcasted_iota(jnp.int32, (c, LANES), 0).astype(F32)
    is_q = lane < dk
    fac_f = jnp.exp(jnp.where(is_q, pos + 1.0, c - 1.0 - pos) * lg)
    fac_b = jnp.exp(jnp.where(is_q, c - pos, pos) * lg)
    rr = lax.broadcasted_iota(jnp.int32, (c, c), 0)
    cc = lax.broadcasted_iota(jnp.int32, (c, c), 1)
    dist = jnp.abs(rr - cc).astype(F32)
    dmat = jnp.exp(dist * lg[:, :1]) * jnp.where(rr == cc, 2.0, 1.0)
    gc = jnp.exp(lg * float(c))

    def local(ci, carry):
        rows = pl.ds(pl.multiple_of(ci * c, c), c)
        x = qk_ref[0, rows, :].astype(F32)
        swapped = jnp.where(lane % 32 < 16, pltpu.roll(x, LANES - 16, 1), pltpu.roll(x, 16, 1))
        qk = x * cos_ref[rows, :] + swapped * sin_ref[rows, :]
        p_ref[ci] = (_dot_nt(qk[:, :dk].astype(BF16), qk[:, dk:].astype(BF16)) * dmat).astype(BF16)
        qk_f = qk * fac_f
        qk_b = qk * fac_b
        q2_ref[rows, :] = jnp.where(is_q, qk_f, pltpu.roll(qk_b, dk, 1)).astype(BF16)
        k2 = jnp.where(is_q, pltpu.roll(qk_f, dk, 1), qk_b).astype(BF16)
        ds_ref[ci] = _dot_tn(k2, v_ref[0, rows, :])
        return carry

    lax.fori_loop(0, n_all, local, 0)

    def step(it, carry):
        s_f, s_b = carry
        cf, cb = _chunk_order(it, n_ctx, n_all)
        sp_ref[cf, 0:dk, :] = s_f.astype(BF16)
        sp_ref[cb, dk:2 * dk, :] = s_b.astype(BF16)
        return gc * s_f + ds_ref[cf, 0:dk, :], gc * s_b + ds_ref[cb, dk:2 * dk, :]

    zero = jnp.zeros((dk, RET_DV), F32)
    lax.fori_loop(0, n_all, step, (zero, zero))

    def readout(ci, carry):
        rows = pl.ds(pl.multiple_of(ci * c, c), c)
        lhs = jnp.concatenate([p_ref[ci], q2_ref[rows, :]], axis=1)
        rhs = jnp.concatenate([v_ref[0, rows, :], sp_ref[ci]], axis=0)
        o = _ln0(_dot(lhs, rhs))
        o_ref[0, rows, :] = (o * _silu(g_ref[0, rows, :].astype(F32))).astype(o_ref.dtype)
        return carry

    lax.fori_loop(0, n_all, readout, 0)


def _retention(z, lg_rows, cos_t, sin_t, ctx_len):
    b, s, _ = z.shape
    c = _pick_tile(math.gcd(ctx_len, s), (RET_CHUNK, 128))
    n = s // c
    kern = functools.partial(_ret_kernel, ctx_len=ctx_len, seq=s, c=c)

    def col(blk):
        return pl.BlockSpec((1, s, LANES), lambda i, h: (i, 0, blk + h))

    return pl.pallas_call(
        kern,
        grid=(b, RET_HEADS),
        in_specs=[pl.BlockSpec((1, 1, LANES), lambda i, h: (h, 0, 0)),
                  col(BLK_RET_QK), col(BLK_RET_V), col(BLK_RET_G),
                  pl.BlockSpec((s, LANES), lambda i, h: (0, 0)),
                  pl.BlockSpec((s, LANES), lambda i, h: (0, 0))],
        out_specs=pl.BlockSpec((1, s, LANES), lambda i, h: (i, 0, h)),
        out_shape=jax.ShapeDtypeStruct((b, s, RET_HEADS * RET_DV), BF16),
        scratch_shapes=[pltpu.VMEM((n, c, c), BF16),
                        pltpu.VMEM((s, LANES), BF16),
                        pltpu.VMEM((n, 2 * RET_DK, RET_DV), F32),
                        pltpu.VMEM((n, 2 * RET_DK, RET_DV), BF16)],
        compiler_params=pltpu.CompilerParams(
            dimension_semantics=("parallel", "parallel"), vmem_limit_bytes=VMEM_LIMIT),
        name="retention",
    )(lg_rows, z, z, z, cos_t, sin_t)


def _hg_kernel(q_ref, ff_ref, fb_ref, i_ref, g_ref, lbf_ref, lbb_ref, nw_ref, a3_ref, m_ref, o_ref,
               acc_ref, q2_ref, ds_ref, sp_ref, dec_ref, *, ctx_len, seq):
    c = GLA_CHUNK
    n_all, n_ctx = seq // c, ctx_len // c
    nl = m_ref.shape[0] - 1
    dk = HG_DK

    def stack4(x_f, x_b):
        return jnp.concatenate([x_f[:c], x_b[:c], x_f[c:], x_b[c:]], axis=0)

    def local(gi, carry):
        rows2 = pl.ds(pl.multiple_of(gi * 2 * c, 2 * c), 2 * c)
        q = _silu(q_ref[0, rows2, :].astype(F32))
        v = i_ref[0, rows2, :]
        kk, r = [], []
        for d, (f_ref, lb_ref) in enumerate(((ff_ref, lbf_ref), (fb_ref, lbb_ref))):
            lb = lb_ref[...]
            f = lb + (1.0 - lb) * jax.nn.sigmoid(f_ref[0, rows2, :].astype(F32))
            g = jnp.log(f)
            kk.append(1.0 - f)
            r.append([_dot(a3_ref[d], jnp.concatenate(_split3(g[j * c:(j + 1) * c]), axis=0))
                      for j in range(2)])

        def rows_of(lo, n=c):
            return [r[d][j][lo:lo + n] for j in range(2) for d in range(2)]

        q4 = stack4(q, q)
        k4 = stack4(kk[0], kk[1])
        scores = m_ref[nl] * _dot_nt(q4.astype(BF16), k4.astype(BF16))
        for l in range(nl):
            e = jnp.exp(jnp.concatenate(rows_of(l * c), axis=0))
            scores = scores + m_ref[l] * _dot_nt((q4 * e).astype(BF16), (k4 * e).astype(BF16))
        v4 = jnp.concatenate([v[:c], v[:c], v[c:], v[c:]], axis=0)
        o4 = _dot(scores.astype(BF16), v4)
        acc_ref[rows2, :] = jnp.concatenate([o4[:c] + o4[c:2 * c], o4[2 * c:3 * c] + o4[3 * c:]], axis=0)

        ecum = [jnp.exp(x) for x in rows_of(nl * c)]
        erest = [jnp.exp(x) for x in rows_of((nl + 1) * c)]
        etot = [jnp.exp(x) for x in rows_of((nl + 2) * c, 8)]
        for j in range(2):
            ci = 2 * gi + j
            qj = q[j * c:(j + 1) * c]
            rows = pl.ds(pl.multiple_of(ci * c, c), c)
            q2_ref[rows, :] = jnp.concatenate([qj * ecum[2 * j], qj * ecum[2 * j + 1]], axis=1).astype(BF16)
            k2 = jnp.concatenate([kk[0][j * c:(j + 1) * c] * erest[2 * j],
                                  kk[1][j * c:(j + 1) * c] * erest[2 * j + 1]], axis=1).astype(BF16)
            ds_ref[ci] = _dot_tn(v[j * c:(j + 1) * c], k2)
            dec_ref[ci] = jnp.concatenate([etot[2 * j], etot[2 * j + 1]], axis=1)
        return carry

    lax.fori_loop(0, n_all // 2, local, 0)

    def step(it, carry):
        st_f, st_b = carry
        cf, cb = _chunk_order(it, n_ctx, n_all)
        sp_ref[cf, :, 0:dk] = st_f.astype(BF16)
        sp_ref[cb, :, dk:2 * dk] = st_b.astype(BF16)
        st_f = st_f * dec_ref[cf][0:1, 0:dk] + ds_ref[cf, :, 0:dk]
        st_b = st_b * dec_ref[cb][0:1, dk:2 * dk] + ds_ref[cb, :, dk:2 * dk]
        return st_f, st_b

    zero = jnp.zeros((HG_DV, dk), F32)
    lax.fori_loop(0, n_all, step, (zero, zero))

    def readout(gi, carry):
        for j in range(2):
            ci = 2 * gi + j
            rows = pl.ds(pl.multiple_of(ci * c, c), c)
            o = acc_ref[rows, :] + _dot_nt(q2_ref[rows, :], sp_ref[ci])
            o = o * lax.rsqrt(jnp.mean(o * o, axis=-1, keepdims=True) + NORM_EPS)
            o_ref[0, rows, :] = (o * nw_ref[...] * _silu(g_ref[0, rows, :].astype(F32))).astype(o_ref.dtype)
        return carry

    lax.fori_loop(0, n_all // 2, readout, 0)


def _hgrn2(z, lb_f, lb_b, norm_w, ctx_len):
    b, s, _ = z.shape
    c = GLA_CHUNK
    n = s // c
    assert n % 2 == 0
    a3, masks = _hg_consts(c)
    a3 = jnp.asarray(a3, BF16)
    masks = jnp.asarray(masks, F32)
    kern = functools.partial(_hg_kernel, ctx_len=ctx_len, seq=s)

    def col(blk):
        return pl.BlockSpec((1, s, LANES), lambda i, h: (i, 0, blk + h))

    def full(a):
        nd = a.ndim
        return pl.BlockSpec(a.shape, lambda i, h: (0,) * nd)

    return pl.pallas_call(
        kern,
        grid=(b, HG_HEADS),
        in_specs=[col(BLK_HG_Q), col(BLK_HG_FF), col(BLK_HG_FB), col(BLK_HG_I), col(BLK_HG_G),
                  pl.BlockSpec((1, LANES), lambda i, h: (0, h)),
                  pl.BlockSpec((1, LANES), lambda i, h: (0, h)),
                  pl.BlockSpec((1, LANES), lambda i, h: (0, 0)),
                  full(a3), full(masks)],
        out_specs=pl.BlockSpec((1, s, LANES), lambda i, h: (i, 0, h)),
        out_shape=jax.ShapeDtypeStruct((b, s, HG_HEADS * HG_DV), BF16),
        scratch_shapes=[pltpu.VMEM((s, HG_DV), F32),
                        pltpu.VMEM((s, 2 * HG_DK), BF16),
                        pltpu.VMEM((n, HG_DV, 2 * HG_DK), F32),
                        pltpu.VMEM((n, HG_DV, 2 * HG_DK), BF16),
                        pltpu.VMEM((n, 8, 2 * HG_DK), F32)],
        compiler_params=pltpu.CompilerParams(
            dimension_semantics=("parallel", "parallel"), vmem_limit_bytes=VMEM_LIMIT),
        name="hgrn2",
    )(z, z, z, z, z, lb_f, lb_b, norm_w, a3, masks)


def _conv_rows(ref, r0, c, seq, ctx_len, w, taps_left):
    lo = pl.multiple_of(jnp.maximum(r0 - HALO, 0), HALO)
    hi = pl.multiple_of(jnp.minimum(r0 + c, seq - HALO), HALO)
    xh = jnp.concatenate([ref[0, pl.ds(lo, HALO), :].astype(F32),
                          ref[0, pl.ds(pl.multiple_of(r0, HALO), c), :].astype(F32),
                          ref[0, pl.ds(hi, HALO), :].astype(F32)], axis=0)
    rows = _row_ids((c + 2 * HALO, 1), r0 - HALO)
    seg_lo = jnp.where(r0 < ctx_len, 0, ctx_len)
    seg_hi = jnp.where(r0 < ctx_len, ctx_len, seq)
    xh = jnp.where((rows >= seg_lo) & (rows < seg_hi), xh, 0.0)
    out = None
    for j in range(w.shape[0]):
        start = HALO + j - taps_left
        term = xh[start:start + c] * w[j:j + 1]
        out = term if out is None else out + term
    return out


def _gdn_kernel(qk_ref, v_ref, g_ref, misc_ref, cw_qk_ref, cw_v_ref, par_ref, nw_ref,
                a3_ref, lv_ref, incl_ref, o_ref,
                qkn_ref, va_ref, acc_ref, lhs1_ref, lhs2_ref, u0_ref, el_ref,
                base_ref, xs_ref, t_ref, rhs_ref, *, ctx_len, seq):
    c = GLA_CHUNK
    cq = 4 * c
    n_all, n_ctx = seq // c, ctx_len // c
    h = pl.program_id(1)
    eye = (lax.broadcasted_iota(jnp.int32, (cq, cq), 0)
           == lax.broadcasted_iota(jnp.int32, (cq, cq), 1)).astype(F32)

    cp = 2 * c if ctx_len % (2 * c) == 0 else c
    lane = lax.broadcasted_iota(jnp.int32, (cp, LANES), 1)
    is_q = lane < GDN_DK

    def prep(ci, carry):
        r0 = ci * cp
        rows = pl.ds(pl.multiple_of(r0, cp), cp)
        qk = _silu(_conv_rows(qk_ref, r0, cp, seq, ctx_len, cw_qk_ref[...], GDN_CONV // 2))
        va = _silu(_conv_rows(v_ref, r0, cp, seq, ctx_len, cw_v_ref[...], GDN_CONV // 2))
        sq = qk * qk
        s_q = jnp.sum(jnp.where(is_q, sq, 0.0), axis=-1, keepdims=True)
        s_k = jnp.sum(jnp.where(is_q, 0.0, sq), axis=-1, keepdims=True)
        inv = lax.rsqrt(jnp.where(is_q, s_q, s_k) + NORM_EPS)
        qkn_ref[rows, :] = qk * inv * jnp.where(is_q, GDN_DK ** -0.5, 1.0)
        va_ref[rows, :] = va
        return carry

    lax.fori_loop(0, seq // cp, prep, 0)

    def stack4(x_f, x_b):
        return jnp.concatenate([x_f[:c], x_b[:c], x_f[c:], x_b[c:]], axis=0)

    rr = lax.broadcasted_iota(jnp.int32, (LANES, 4 * LANES), 0)
    cc4 = lax.broadcasted_iota(jnp.int32, (LANES, 4 * LANES), 1) // LANES
    pick = (rr == cc4 * GDN_HEADS + h).astype(BF16)
    lane = lax.broadcasted_iota(jnp.int32, (cq, LANES), 1)
    low = lane < GDN_DK
    rowb = lax.broadcasted_iota(jnp.int32, (cq, LANES), 0) // c
    own_half = (rowb % 2 == 0) == low
    e_hi = (lane == 0).astype(BF16)
    eye_k = (lax.broadcasted_iota(jnp.int32, (GDN_DK, LANES), 0)
             == lax.broadcasted_iota(jnp.int32, (GDN_DK, LANES), 1)).astype(BF16)
    half_t = [lax.broadcasted_iota(jnp.int32, (GDN_DK, LANES), 1) < GDN_DK,
              lax.broadcasted_iota(jnp.int32, (GDN_DK, LANES), 1) >= GDN_DK]

    def local_quad(gi):
        rows2 = pl.ds(pl.multiple_of(gi * 2 * c, 2 * c), 2 * c)
        qkn = qkn_ref[rows2, :]
        va = va_ref[rows2, :]
        ab = _dot(misc_ref[0, rows2, :], pick)
        loga, beta = [], []
        for d in range(2):
            a_log = par_ref[0, 2 * d:2 * d + 1, :]
            dt_b = par_ref[0, 2 * d + 1:2 * d + 2, :]
            loga.append(-jnp.exp(a_log) * jax.nn.softplus(ab[:, d * LANES:(d + 1) * LANES] + dt_b))
            beta.append(jax.nn.sigmoid(ab[:, (2 + d) * LANES:(3 + d) * LANES]))
        loga = stack4(loga[0], loga[1])
        beta = stack4(beta[0], beta[1])
        qk4 = stack4(qkn, qkn)
        kq4 = pltpu.roll(qk4, GDN_DK, 1)
        q2 = jnp.where(low, qk4, kq4)
        k2 = jnp.where(low, kq4, qk4)
        v4 = stack4(va, va)

        hi, mid, _ = _split3(loga)
        r = _dot(a3_ref[...], jnp.concatenate([hi, mid], axis=0))
        cum, total = r[:cq], r[cq:]
        chi, cmid, _ = _split3(cum)
        cum_row = _dot_nt(jnp.concatenate([e_hi, e_hi], axis=1),
                          jnp.concatenate([chi, cmid], axis=1))
        dec = jnp.exp(jnp.minimum(jnp.concatenate([cum, cum], axis=1) - cum_row, 0.0))
        k_one = jnp.where(low, k2, 0.0).astype(BF16)
        kq = _dot_nt(jnp.concatenate([k2.astype(BF16), q2.astype(BF16)], axis=0), k_one)
        base = (jnp.concatenate([beta, beta], axis=1) * dec) * kq[:cq]
        base_ref[gi] = base
        xs_ref[gi] = eye - base * lv_ref[0]
        ecum = jnp.exp(cum)
        rhs_ref[gi] = jnp.concatenate([beta * ecum * k2, beta * v4], axis=1).astype(BF16)
        qkd = (kq[cq:] * (incl_ref[...] * dec)).astype(BF16)
        qt = jnp.where(own_half, q2 * ecum, 0.0).astype(BF16)
        kt = (k2 * jnp.exp(total - cum)).astype(BF16)
        ktt = _dot_nt(eye_k, kt)
        etot = jnp.exp(total)
        for j in range(2):
            for d in range(2):
                lo_r = (2 * j + d) * c
                st = step_of(2 * gi + j, d)
                lhs1_ref[st, (2 + d) * c:(3 + d) * c, :] = qt[lo_r:lo_r + c]
                lhs2_ref[st, d * c:(d + 1) * c, :] = qkd[lo_r:lo_r + c, j * LANES:(j + 1) * LANES]
                lhs2_ref[st, (2 + d) * c:(3 + d) * c, :] = jnp.where(
                    half_t[d], ktt[:, j * LANES:(j + 1) * LANES], 0.0).astype(BF16)
                el_ref[d, st] = etot[lo_r:lo_r + 8]

    def step_of(ci, d):
        return ci if d == 0 else jnp.where(ci < n_ctx, n_ctx - 1 - ci, n_all + n_ctx - 1 - ci)

    nq = n_all // 2

    def over_stacks(fn, sizes):
        group = _pick_tile(nq, sizes)

        def body(gj, carry):
            for k in range(group):
                fn(gj * group + k)
            return carry
        lax.fori_loop(0, nq // group, body, 0)

    over_stacks(local_quad, (GDN_GROUP, 1))

    for l in range(1, lv_ref.shape[0]):
        def left(gi, l=l):
            nm = (base_ref[gi] * lv_ref[l]).astype(BF16)
            t_ref[gi] = _dot(xs_ref[gi].astype(BF16), nm).astype(BF16)

        def right(gi):
            x = xs_ref[gi]
            xs_ref[gi] = x - _dot(t_ref[gi], x.astype(BF16))

        over_stacks(left, (2 * GDN_GROUP, GDN_GROUP, 1))
        over_stacks(right, (2 * GDN_GROUP, GDN_GROUP, 1))

    def finish(gi):
        wu = _dot(xs_ref[gi].astype(BF16), rhs_ref[gi])
        w = jnp.where(own_half, wu[:, :LANES], 0.0).astype(BF16)
        for j in range(2):
            for d in range(2):
                lo_r = (2 * j + d) * c
                st = step_of(2 * gi + j, d)
                lhs1_ref[st, d * c:(d + 1) * c, :] = w[lo_r:lo_r + c]
                u0_ref[st, d * c:(d + 1) * c, :] = wu[lo_r:lo_r + c, LANES:]

    over_stacks(finish, (GDN_GROUP, 1))

    is_fwd_row = lax.broadcasted_iota(jnp.int32, (2 * GDN_DK, LANES), 0) < GDN_DK

    def step(it, s2):
        cf, cb = _chunk_order(it, n_ctx, n_all)
        r1 = _dot(lhs1_ref[it], s2.astype(BF16))
        u = u0_ref[it] - r1[:2 * c]
        r2 = _dot(lhs2_ref[it], u.astype(BF16))
        o2 = r1[2 * c:] + r2[:2 * c]
        acc_ref[0, pl.ds(pl.multiple_of(cf * c, c), c), :] = o2[:c]
        acc_ref[1, pl.ds(pl.multiple_of(cb * c, c), c), :] = o2[c:]
        el2 = jnp.where(is_fwd_row, el_ref[0, it][0:1], el_ref[1, it][0:1])
        return el2 * s2 + r2[2 * c:]

    lax.fori_loop(0, n_all, step, jnp.zeros((2 * GDN_DK, GDN_DV), F32))

    def readout(ci, carry):
        rows = pl.ds(pl.multiple_of(ci * cp, cp), cp)
        o = acc_ref[0, rows, :] + acc_ref[1, rows, :]
        o = o * lax.rsqrt(jnp.mean(o * o, axis=-1, keepdims=True) + NORM_EPS)
        o_ref[0, rows, :] = (o * nw_ref[...] * _silu(g_ref[0, rows, :].astype(F32))).astype(o_ref.dtype)
        return carry

    lax.fori_loop(0, seq // cp, readout, 0)


def _gdn(z, conv_w, par, norm_w, ctx_len):
    b, s, _ = z.shape
    c = GLA_CHUNK
    n = s // c
    assert n % 2 == 0
    a3, lv, incl = _gdn_consts(c)
    a3 = jnp.asarray(np.tile(a3, (1, 2)), BF16)
    lv = jnp.asarray(lv, F32)
    incl = jnp.asarray(incl, F32)
    kern = functools.partial(_gdn_kernel, ctx_len=ctx_len, seq=s)

    def col(blk):
        return pl.BlockSpec((1, s, LANES), lambda i, h: (i, 0, blk + h))

    def full(a):
        nd = a.ndim
        return pl.BlockSpec(a.shape, lambda i, h: (0,) * nd)

    return pl.pallas_call(
        kern,
        grid=(b, GDN_HEADS),
        in_specs=[col(BLK_GDN_QK), col(BLK_GDN_V), col(BLK_GDN_G),
                  pl.BlockSpec((1, s, LANES), lambda i, h: (i, 0, BLK_GDN_MISC)),
                  pl.BlockSpec((GDN_CONV, LANES), lambda i, h: (0, h)),
                  pl.BlockSpec((GDN_CONV, LANES), lambda i, h: (0, GDN_HEADS + h)),
                  pl.BlockSpec((1, 8, LANES), lambda i, h: (h, 0, 0)),
                  pl.BlockSpec((1, LANES), lambda i, h: (0, 0)),
                  full(a3), full(lv), full(incl)],
        out_specs=pl.BlockSpec((1, s, LANES), lambda i, h: (i, 0, h)),
        out_shape=jax.ShapeDtypeStruct((b, s, GDN_HEADS * GDN_DV), BF16),
        scratch_shapes=[pltpu.VMEM((s, LANES), F32),
                        pltpu.VMEM((s, LANES), F32),
                        pltpu.VMEM((2, s, LANES), F32),
                        pltpu.VMEM((n, 4 * c, 2 * GDN_DK), BF16),
                        pltpu.VMEM((n, 4 * c, 2 * c), BF16),
                        pltpu.VMEM((n, 2 * c, GDN_DV), F32),
                        pltpu.VMEM((2, n, 8, LANES), F32),
                        pltpu.VMEM((n // 2, 4 * c, 4 * c), F32),
                        pltpu.VMEM((n // 2, 4 * c, 4 * c), F32),
                        pltpu.VMEM((n // 2, 4 * c, 4 * c), BF16),
                        pltpu.VMEM((n // 2, 4 * c, LANES + GDN_DV), BF16)],
        compiler_params=pltpu.CompilerParams(
            dimension_semantics=("parallel", "parallel"), vmem_limit_bytes=VMEM_LIMIT),
        name="gated_deltanet",
    )(z, z, z, z, conv_w, conv_w, par, norm_w, a3, lv, incl)


def _lru_kernel(x_ref, gate_ref, cw_ref, cb_ref, wd_ref, bd_ref, lam_ref, o_ref,
                af_ref, uf_ref, ab_ref, ub_ref, hf_ref, hb_ref, *, ctx_len, seq, c):
    w = LRU_WIDTH
    n_all, n_ctx = seq // c, ctx_len // c

    def gates(ci, d, a_ref, u_ref):
        r0 = ci * c
        xc = _conv_rows(x_ref, r0, c, seq, ctx_len, cw_ref[...], LRU_CONV // 2) + cb_ref[...]
        y = _dot(xc.astype(BF16), wd_ref[d]) + bd_ref[d]
        r = jax.nn.sigmoid(y[:, :w])
        i = jax.nn.sigmoid(y[:, w:])
        log_a = -LRU_C * r * jax.nn.softplus(-lam_ref[d])
        a = jnp.exp(log_a)
        a_ref[...] = a
        u_ref[...] = jnp.sqrt(1.0 - a * a) * (i * xc)

    def step(it, carry):
        h_f, h_b = carry
        cf, cb = _chunk_order(it, n_ctx, n_all)
        gates(cf, 0, af_ref, uf_ref)
        gates(cb, 1, ab_ref, ub_ref)
        base_f = cf * c
        base_b = cb * c

        def row(t, hh):
            hf, hb = hh
            tb = c - 1 - t
            hf = af_ref[pl.ds(t, 1), :] * hf + uf_ref[pl.ds(t, 1), :]
            hb = ab_ref[pl.ds(tb, 1), :] * hb + ub_ref[pl.ds(tb, 1), :]
            hf_ref[pl.ds(base_f + t, 1), :] = hf
            hb_ref[pl.ds(base_b + tb, 1), :] = hb
            return hf, hb

        return lax.fori_loop(0, c, row, (h_f, h_b), unroll=8)

    zero = jnp.zeros((1, w), F32)
    lax.fori_loop(0, n_all, step, (zero, zero))

    def readout(ci, carry):
        rows = pl.ds(pl.multiple_of(ci * c, c), c)
        hsum = hf_ref[rows, :] + hb_ref[rows, :]
        o_ref[0, rows, :] = (jax.nn.gelu(gate_ref[0, rows, :].astype(F32)) * hsum).astype(o_ref.dtype)
        return carry

    lax.fori_loop(0, n_all, readout, 0)


def _rglru(z, conv_w, conv_b, w_dense, b_dense, lam, ctx_len):
    b, s, _ = z.shape
    w = LRU_WIDTH
    c = _pick_tile(math.gcd(ctx_len, s), (LRU_ROWS, 128, 64))
    kern = functools.partial(_lru_kernel, ctx_len=ctx_len, seq=s, c=c)

    def full(a):
        nd = a.ndim
        return pl.BlockSpec(a.shape, lambda i: (0,) * nd)

    return pl.pallas_call(
        kern,
        grid=(b,),
        in_specs=[pl.BlockSpec((1, s, w), lambda i: (i, 0, BLK_LRU_X * LANES // w)),
                  pl.BlockSpec((1, s, w), lambda i: (i, 0, BLK_LRU_G * LANES // w)),
                  full(conv_w), full(conv_b), full(w_dense), full(b_dense), full(lam)],
        out_specs=pl.BlockSpec((1, s, w), lambda i: (i, 0, 0)),
        out_shape=jax.ShapeDtypeStruct((b, s, w), BF16),
        scratch_shapes=[pltpu.VMEM((c, w), F32)] * 4 + [pltpu.VMEM((s, w), F32)] * 2,
        compiler_params=pltpu.CompilerParams(
            dimension_semantics=("parallel",), vmem_limit_bytes=VMEM_LIMIT),
        name="rglru",
    )(z, z, conv_w, conv_b, w_dense, b_dense, lam)


def _merge_kernel(x_ref, b0_ref, b1_ref, b2_ref, b3_ref, gate_ref, gl_ref, gc_ref, wb_ref, wo_ref,
                  lng_ref, lnb_ref, o_ref, *, ctx_len, tm, alpha):
    t = pl.program_id(1)
    merged = None
    for j, br in enumerate((b0_ref, b1_ref, b2_ref, b3_ref)):
        gate = jax.nn.sigmoid(gate_ref[0, :, j * D_MODEL:(j + 1) * D_MODEL].astype(F32))
        term = gate * _dot(br[0], wb_ref[j])
        merged = term if merged is None else merged + term
    y = _dot(merged.astype(BF16), wo_ref[...])
    rows = _row_ids((tm, 1), t * tm)
    g1 = jnp.where(rows < ctx_len, gc_ref[...], gl_ref[0])
    o_ref[0] = _ln0(alpha * x_ref[0] + g1 * y) * lng_ref[...] + lnb_ref[...]


def _merge(h, branches, z, g_l, g_c, w_branch, w_out, ln_g, ln_b, ctx_len, alpha):
    b, s, d = h.shape
    tm = _pick_tile(s, (256, 128))
    kern = functools.partial(_merge_kernel, ctx_len=ctx_len, tm=tm, alpha=alpha)
    bw = BRANCH_WIDTH

    def tile(width, blk=0):
        return pl.BlockSpec((1, tm, width), lambda i, t: (i, t, blk))

    def full(a):
        nd = a.ndim
        return pl.BlockSpec(a.shape, lambda i, t: (0,) * nd)

    return pl.pallas_call(
        kern,
        grid=(b, s // tm),
        in_specs=[tile(d), tile(bw), tile(bw), tile(bw), tile(bw), tile(N_BRANCH * d, BLK_MG),
                  pl.BlockSpec((1, 1, d), lambda i, t: (i, 0, 0)), full(g_c),
                  full(w_branch), full(w_out), full(ln_g), full(ln_b)],
        out_specs=tile(d),
        out_shape=jax.ShapeDtypeStruct((b, s, d), F32),
        compiler_params=pltpu.CompilerParams(
            dimension_semantics=("parallel", "parallel"), vmem_limit_bytes=VMEM_LIMIT),
        name="merge",
    )(h, *branches, z, g_l, g_c, w_branch, w_out, ln_g, ln_b)


FFN_ROWS = 64
FFN_COLS = 256


def _ffn_kernel(x_ref, xp_ref, xn_ref, shl_ref, scl_ref, gl_ref, shc_ref, scc_ref, gc_ref,
                wu_ref, cw_ref, cb_ref, wd_ref, lng_ref, lnb_ref,
                o_ref, xm_ref, uv_ref, ug_ref, act_ref, *, ctx_len, seq, tm, alpha):
    d_ff = wd_ref.shape[0]
    fc = _pick_tile(d_ff, (FFN_COLS, LANES))
    rb = _pick_tile(tm, (FFN_ROWS, 8))
    r0 = pl.program_id(1) * tm

    x = x_ref[0]
    x_ext = jnp.concatenate([xp_ref[0], x, xn_ref[0]], axis=0)
    xm = _modulated(x_ext, _row_ids((tm + 16, 1), r0 - 8), ctx_len,
                    shl_ref[0], scl_ref[0], shc_ref[...], scc_ref[...])
    xm_ref[...] = xm.astype(BF16)

    rows = _row_ids((tm, 1), r0)
    m_up = jnp.where((rows == 0) | (rows == ctx_len), 0.0, 1.0)
    m_dn = jnp.where((rows == ctx_len - 1) | (rows == seq - 1), 0.0, 1.0)

    def conv(u_ref, a, mu, md, cw, cb):
        up = u_ref[7 + a:7 + a + rb, :]
        mid = u_ref[8 + a:8 + a + rb, :]
        dn = u_ref[9 + a:9 + a + rb, :]
        return cw[0:1] * (mu * up) + cw[1:2] * mid + cw[2:3] * (md * dn) + cb

    for j in range(d_ff // fc):
        cv = slice(j * fc, (j + 1) * fc)
        cg = slice(d_ff + j * fc, d_ff + (j + 1) * fc)
        uv_ref[...] = _dot(xm_ref[...], wu_ref[:, cv])
        ug_ref[...] = _dot(xm_ref[...], wu_ref[:, cg])
        for a in range(0, tm, rb):
            mu, md = m_up[a:a + rb], m_dn[a:a + rb]
            val = conv(uv_ref, a, mu, md, cw_ref[:, cv], cb_ref[:, cv])
            gate = conv(ug_ref, a, mu, md, cw_ref[:, cg], cb_ref[:, cg])
            act_ref[a:a + rb, cv] = (_silu(gate) * val).astype(BF16)

    f = _dot(act_ref[...], wd_ref[...])
    g2 = jnp.where(rows < ctx_len, gc_ref[...], gl_ref[0])
    o_ref[0] = _ln0(alpha * x + g2 * f) * lng_ref[...] + lnb_ref[...]


def _conv_ffn(h, mods_l, mods_c, w_up, conv_w, conv_b, w_down, ln_g, ln_b, ctx_len, alpha):
    b, s, d = h.shape
    d_ff = w_down.shape[0]
    tm = _pick_tile(s, (768, 384, 256, 128))
    fc = _pick_tile(d_ff, (FFN_COLS, LANES))
    kern = functools.partial(_ffn_kernel, ctx_len=ctx_len, seq=s, tm=tm, alpha=alpha)
    nb8 = s // 8

    def vec_l():
        return pl.BlockSpec((1, 1, d), lambda i, t: (i, 0, 0))

    def resident(a):
        nd = a.ndim
        return pl.BlockSpec(a.shape, lambda i, t: (0,) * nd, pipeline_mode=pl.Buffered(1))

    sh_l, sc_l, g_l = mods_l
    sh_c, sc_c, g_c = mods_c
    return pl.pallas_call(
        kern,
        grid=(b, s // tm),
        in_specs=[pl.BlockSpec((1, tm, d), lambda i, t: (i, t, 0)),
                  pl.BlockSpec((1, 8, d), lambda i, t: (i, jnp.maximum(t * (tm // 8) - 1, 0), 0)),
                  pl.BlockSpec((1, 8, d), lambda i, t: (i, jnp.minimum((t + 1) * (tm // 8), nb8 - 1), 0)),
                  vec_l(), vec_l(), vec_l(), resident(sh_c), resident(sc_c), resident(g_c),
                  resident(w_up), resident(conv_w), resident(conv_b), resident(w_down),
                  resident(ln_g), resident(ln_b)],
        out_specs=pl.BlockSpec((1, tm, d), lambda i, t: (i, t, 0)),
        out_shape=jax.ShapeDtypeStruct((b, s, d), F32),
        scratch_shapes=[pltpu.VMEM((tm + 16, d), BF16),
                        pltpu.VMEM((tm + 16, fc), F32), pltpu.VMEM((tm + 16, fc), F32),
                        pltpu.VMEM((tm, d_ff), BF16)],
        compiler_params=pltpu.CompilerParams(
            dimension_semantics=("parallel", "parallel"), vmem_limit_bytes=VMEM_LIMIT),
        name="conv_ffn",
    )(h, h, h, sh_l, sc_l, g_l, sh_c, sc_c, g_c, w_up, conv_w, conv_b, w_down, ln_g, ln_b)


def _rope_tables(seq, ctx_len):
    quarter = RET_DK // 4
    inv_freq = ROPE_BASE ** (-jnp.arange(quarter, dtype=F32) / quarter)
    p = jnp.arange(seq - ctx_len)
    row = (p // GRID_W).astype(F32)
    col = (p % GRID_W).astype(F32)
    ang_r = row[:, None] * inv_freq[None, :]
    ang_c = col[:, None] * inv_freq[None, :]
    cos = jnp.concatenate([jnp.cos(ang_r)] * 2 + [jnp.cos(ang_c)] * 2, axis=-1)
    sin = jnp.concatenate([-jnp.sin(ang_r), jnp.sin(ang_r), -jnp.sin(ang_c), jnp.sin(ang_c)], axis=-1)
    cos = jnp.concatenate([jnp.ones((ctx_len, RET_DK), F32), cos], axis=0)
    sin = jnp.concatenate([jnp.zeros((ctx_len, RET_DK), F32), sin], axis=0)
    ks = RET_DK ** -0.5
    return jnp.concatenate([cos, cos * ks], axis=-1), jnp.concatenate([sin, sin * ks], axis=-1)


def _block_diag(w):
    nb, k, _ = w.shape
    eye = jnp.eye(nb, dtype=w.dtype)
    return (eye[:, None, :, None] * w[:, :, None, :]).reshape(nb * k, nb * k)


def kernel(x, c, ctx, c_ctx, w_mod, b_mod, w_in, hg_lb, hg_norm, gdn_conv, gdn_a_log, gdn_dt_bias,
           gdn_norm, lru_conv, lru_conv_b, lru_w_a, lru_b_a, lru_w_i, lru_b_i, lru_lam, w_branch,
           w_out, ln_mix_g, ln_mix_b, w_up, ffn_conv, ffn_conv_b, w_down, ln_ffn_g, ln_ffn_b):
    depth = w_in.shape[0]
    batch, seq_l, d = x.shape
    ctx_len = ctx.shape[1]
    seq = ctx_len + seq_l
    alpha = (2.0 * depth) ** 0.25

    perm, n_in = _proj_perm()
    w_in_p = _take_columns(w_in.astype(BF16), perm, n_in)
    gdn_conv_p = _take_columns(gdn_conv, _gdn_conv_perm(), gdn_conv.shape[-1])
    lb = jnp.cumsum(jax.nn.softmax(hg_lb.astype(F32), axis=0), axis=0)
    lb = lb - lb[:1]
    log_gamma = jnp.log1p(-jnp.exp2(-5.0 - jnp.arange(RET_HEADS, dtype=F32)))
    lg_rows = jnp.broadcast_to(log_gamma[:, None, None], (RET_HEADS, 1, LANES))
    cos_t, sin_t = _rope_tables(seq, ctx_len)
    w_branch_b = w_branch.astype(BF16)
    w_out_b = w_out.astype(BF16)
    w_up_b = w_up.astype(BF16)
    w_down_b = w_down.astype(BF16)

    cc = jnp.concatenate([c, c_ctx[None, :], jnp.zeros((7, d), F32)], axis=0)
    mod = _modulation(cc, w_mod, b_mod)

    h = jnp.concatenate([ctx, x], axis=1)
    for i in range(depth):
        ml = mod[i, :batch].reshape(batch, N_MOD, 1, d)
        mc = mod[i, batch].reshape(N_MOD, 1, d)
        sh1, sc1, g1, sh2, sc2, g2 = (ml[:, k] for k in range(N_MOD))
        csh1, csc1, cg1, csh2, csc2, cg2 = (mc[k] for k in range(N_MOD))

        z = _in_projection(h, sh1, sc1, csh1, csc1, w_in_p[i], ctx_len)

        ret = _retention(z, lg_rows, cos_t, sin_t, ctx_len)
        hg = _hgrn2(z, lb[i, 0][None, :], lb[i, 1][None, :], hg_norm[i][None, :], ctx_len)
        par = jnp.stack([gdn_a_log[i, 0], gdn_dt_bias[i, 0], gdn_a_log[i, 1], gdn_dt_bias[i, 1]], axis=1)
        par = jnp.broadcast_to(jnp.pad(par, ((0, 0), (0, 4)))[:, :, None], (GDN_HEADS, 8, LANES))
        gd = _gdn(z, gdn_conv_p[i], par, gdn_norm[i][None, :], ctx_len)
        w_dense = jnp.stack([
            jnp.concatenate([_block_diag(lru_w_a[i, dd]), _block_diag(lru_w_i[i, dd])], axis=1)
            for dd in range(2)]).astype(BF16)
        b_dense = jnp.stack([jnp.concatenate([lru_b_a[i, dd], lru_b_i[i, dd]])[None, :] for dd in range(2)])
        lr = _rglru(z, lru_conv[i], lru_conv_b[i][None, :], w_dense, b_dense,
                    lru_lam[i].astype(F32)[:, None, :], ctx_len)

        h = _merge(h, (ret, hg, gd, lr), z, g1, cg1, w_branch_b[i], w_out_b[i],
                   ln_mix_g[i][None, :], ln_mix_b[i][None, :], ctx_len, alpha)
        h = _conv_ffn(h, (sh2, sc2, g2), (csh2, csc2, cg2), w_up_b[i], ffn_conv[i],
                      ffn_conv_b[i][None, :], w_down_b[i], ln_ffn_g[i][None, :], ln_ffn_b[i][None, :],
                      ctx_len, alpha)
    return h[:, ctx_len:]
```

```python
import functools
import math

import numpy as np
import jax
import jax.numpy as jnp
from jax import lax
from jax.experimental import pallas as pl
from jax.experimental.pallas import tpu as pltpu

F32 = jnp.float32
BF16 = jnp.bfloat16

D_MODEL = 1024
GRID_W = 64
NORM_EPS = 1e-6
ROPE_BASE = 10000.0
RET_HEADS, RET_DK, RET_DV = 4, 64, 128
HG_HEADS, HG_DK, HG_DV = 4, 128, 128
GDN_HEADS, GDN_DK, GDN_DV, GDN_CONV = 4, 64, 128, 4
LRU_WIDTH, LRU_BLOCKS, LRU_CONV, LRU_C = 512, 8, 4, 8.0
N_BRANCH, BRANCH_WIDTH = 4, 512
D_FF, FFN_CONV = 2816, 3
N_MOD = 6

LANES = 128
HALO = 16
VMEM_LIMIT = 56 * 1024 * 1024

BLK_MG = 0
BLK_LRU_X = 32
BLK_LRU_G = 36
BLK_HG_Q = 40
BLK_HG_FF = 44
BLK_HG_FB = 48
BLK_HG_I = 52
BLK_HG_G = 56
BLK_RET_QK = 60
BLK_RET_V = 64
BLK_RET_G = 68
BLK_GDN_QK = 72
BLK_GDN_V = 76
BLK_GDN_G = 80
BLK_GDN_MISC = 84
N_BLK = 85
N_PROJ = N_BLK * LANES

RET_CHUNK = 256
GLA_CHUNK = 64
LRU_ROWS = 256
GDN_GROUP = 3


def _dot(a, b):
    return jnp.dot(a, b, preferred_element_type=F32)


def _dot_nt(a, b):
    return lax.dot_general(a, b, (((1,), (1,)), ((), ())), preferred_element_type=F32)


def _dot_tn(a, b):
    return lax.dot_general(a, b, (((0,), (0,)), ((), ())), preferred_element_type=F32)


def _silu(x):
    return x * jax.nn.sigmoid(x)


def _split3(x):
    hi = x.astype(BF16)
    r1 = x - hi.astype(F32)
    mid = r1.astype(BF16)
    lo = (r1 - mid.astype(F32)).astype(BF16)
    return hi, mid, lo


def _ln0(x):
    mu = jnp.mean(x, axis=-1, keepdims=True)
    xc = x - mu
    var = jnp.mean(xc * xc, axis=-1, keepdims=True)
    return xc * lax.rsqrt(var + NORM_EPS)


def _row_ids(shape, start):
    return start + lax.broadcasted_iota(jnp.int32, shape, 0)


def _chunk_order(it, n_ctx, n_all):
    cb = jnp.where(it < n_ctx, n_ctx - 1 - it, n_all + n_ctx - 1 - it)
    return it, cb


def _level_structs(c):
    t = np.arange(c)
    j = t[None, :]
    segs, masks = [], []
    half = c // 2
    while half >= 1:
        blk = t // (2 * half)
        ref = blk * 2 * half + half - 1
        upper = (t - blk * 2 * half) >= half
        seg_u = (j > ref[:, None]) & (j <= t[:, None])
        seg_l = (j > t[:, None]) & (j <= ref[:, None])
        segs.append(np.where(upper[:, None], seg_u, seg_l))
        masks.append((blk[:, None] == blk[None, :]) & upper[:, None] & (~upper)[None, :])
        half //= 2
    return segs, masks


def _cum_rows(c):
    t = np.arange(c)
    j = t[None, :]
    incl = j <= t[:, None]
    rest = j > t[:, None]
    ones = np.ones((8, c), bool)
    return [incl, rest, ones]


def _flip2(m):
    return m[::-1, ::-1]


@functools.lru_cache(maxsize=None)
def _hg_consts(c):
    segs, masks = _level_structs(c)
    rows_f = segs + _cum_rows(c)
    rows_b = [_flip2(m) for m in segs] + [_flip2(m) for m in _cum_rows(c)[:2]] + _cum_rows(c)[2:]
    masks = masks + [np.eye(c, dtype=bool)]
    a_f = np.concatenate(rows_f, axis=0).astype(np.float32)
    a_b = np.concatenate(rows_b, axis=0).astype(np.float32)
    a3 = np.stack([np.tile(a_f, (1, 3)), np.tile(a_b, (1, 3))])

    def bd4(m_f):
        out = np.zeros((4 * c, 4 * c), np.float32)
        for b in range(4):
            out[b * c:(b + 1) * c, b * c:(b + 1) * c] = m_f if b % 2 == 0 else _flip2(m_f)
        return out

    return a3, np.stack([bd4(m) for m in masks])


@functools.lru_cache(maxsize=None)
def _gdn_consts(c):
    _, masks = _level_structs(c)
    masks = masks[::-1]
    t = np.arange(c)
    incl = t[None, :] <= t[:, None]

    def bd4(m_f):
        out = np.zeros((4 * c, 4 * c), np.float32)
        for b in range(4):
            out[b * c:(b + 1) * c, b * c:(b + 1) * c] = m_f if b % 2 == 0 else _flip2(m_f)
        return out

    ones = np.ones((c, c), bool)
    a = np.concatenate([bd4(incl), bd4(ones)], axis=0)
    lv = np.stack([bd4(m) for m in masks])
    return a, lv, bd4(incl)


def _proj_perm():
    names = (('ret_q', 256), ('ret_k', 256), ('ret_v', 512), ('ret_g', 512),
             ('hg_q', 512), ('hg_f_fwd', 512), ('hg_f_bwd', 512), ('hg_i', 512), ('hg_g', 512),
             ('gdn_qkv', 1024), ('gdn_a', 8), ('gdn_b', 8), ('gdn_g', 512),
             ('lru_x', 512), ('lru_gate', 512), ('merge_gate', 4096))
    off, o = {}, 0
    for name, w in names:
        off[name] = o
        o += w
    n_in = o
    perm = np.full((N_PROJ,), n_in, np.int64)

    def put(blk, src, width):
        perm[blk * LANES: blk * LANES + width] = np.arange(src, src + width)

    put(BLK_MG, off['merge_gate'], 4096)
    put(BLK_LRU_X, off['lru_x'], 512)
    put(BLK_LRU_G, off['lru_gate'], 512)
    put(BLK_HG_Q, off['hg_q'], 512)
    put(BLK_HG_FF, off['hg_f_fwd'], 512)
    put(BLK_HG_FB, off['hg_f_bwd'], 512)
    put(BLK_HG_I, off['hg_i'], 512)
    put(BLK_HG_G, off['hg_g'], 512)
    for h in range(RET_HEADS):
        perm[(BLK_RET_QK + h) * LANES: (BLK_RET_QK + h) * LANES + 64] = off['ret_q'] + 64 * h + np.arange(64)
        perm[(BLK_RET_QK + h) * LANES + 64: (BLK_RET_QK + h + 1) * LANES] = off['ret_k'] + 64 * h + np.arange(64)
    put(BLK_RET_V, off['ret_v'], 512)
    put(BLK_RET_G, off['ret_g'], 512)
    gq = off['gdn_qkv']
    for h in range(GDN_HEADS):
        perm[(BLK_GDN_QK + h) * LANES: (BLK_GDN_QK + h) * LANES + 64] = gq + 64 * h + np.arange(64)
        perm[(BLK_GDN_QK + h) * LANES + 64: (BLK_GDN_QK + h + 1) * LANES] = gq + 256 + 64 * h + np.arange(64)
    put(BLK_GDN_V, gq + 512, 512)
    put(BLK_GDN_G, off['gdn_g'], 512)
    put(BLK_GDN_MISC, off['gdn_a'], 8)
    perm[BLK_GDN_MISC * LANES + 8: BLK_GDN_MISC * LANES + 16] = off['gdn_b'] + np.arange(8)
    return perm, n_in


def _take_columns(w, perm, n_src):
    pieces, start = [], 0
    for i in range(1, len(perm) + 1):
        pad = perm[start] == n_src
        if i < len(perm) and ((pad and perm[i] == n_src) or
                              (not pad and perm[i] != n_src and perm[i] == perm[i - 1] + 1)):
            continue
        if pad:
            pieces.append(jnp.zeros(w.shape[:-1] + (i - start,), w.dtype))
        else:
            pieces.append(w[..., int(perm[start]):int(perm[start]) + i - start])
        start = i
    return jnp.concatenate(pieces, axis=-1)


def _gdn_conv_perm():
    p = np.zeros((1024,), np.int64)
    for h in range(GDN_HEADS):
        p[128 * h: 128 * h + 64] = 64 * h + np.arange(64)
        p[128 * h + 64: 128 * h + 128] = 256 + 64 * h + np.arange(64)
    p[512:] = 512 + np.arange(512)
    return p


def _mod_kernel(c_ref, w_ref, b_ref, o_ref):
    s = _silu(c_ref[...])
    o_ref[0] = jnp.dot(s, w_ref[0], preferred_element_type=F32,
                       precision=lax.Precision.HIGHEST) + b_ref[0]


def _modulation(cc, w_mod, b_mod):
    depth, d, n = w_mod.shape
    rows = cc.shape[0]
    tn = 1024
    return pl.pallas_call(
        _mod_kernel,
        grid=(depth, n // tn),
        in_specs=[pl.BlockSpec((rows, d), lambda l, j: (0, 0)),
                  pl.BlockSpec((1, d, tn), lambda l, j: (l, 0, j)),
                  pl.BlockSpec((1, 1, tn), lambda l, j: (l, 0, j))],
        out_specs=pl.BlockSpec((1, rows, tn), lambda l, j: (l, 0, j)),
        out_shape=jax.ShapeDtypeStruct((depth, rows, n), F32),
        compiler_params=pltpu.CompilerParams(vmem_limit_bytes=VMEM_LIMIT),
        name="modulation",
    )(cc, w_mod, b_mod.reshape(depth, 1, n))


def _modulated(x, rows, ctx_len, sh_l, sc_l, sh_c, sc_c):
    is_ctx = rows < ctx_len
    scale = jnp.where(is_ctx, sc_c, sc_l)
    shift = jnp.where(is_ctx, sh_c, sh_l)
    return _ln0(x) * (1.0 + scale) + shift


def _inproj_kernel(x_ref, shl_ref, scl_ref, shc_ref, scc_ref, w_ref, o_ref, xm_ref, *, ctx_len, tm):
    t = pl.program_id(1)

    @pl.when(pl.program_id(2) == 0)
    def _():
        rows = _row_ids((tm, 1), t * tm)
        xm = _modulated(x_ref[0], rows, ctx_len, shl_ref[0], scl_ref[0], shc_ref[...], scc_ref[...])
        xm_ref[...] = xm.astype(BF16)

    o_ref[0] = _dot(xm_ref[...], w_ref[...]).astype(o_ref.dtype)


def _pick_tile(n, cands):
    for c in cands:
        if n % c == 0:
            return c
    return n


def _in_projection(h, sh_l, sc_l, sh_c, sc_c, w_p, ctx_len):
    b, s, d = h.shape
    n = w_p.shape[1]
    tm = _pick_tile(s, (1152, 768, 512, 384, 256, 128))
    tn = _pick_tile(n, (2176, 1280, 640, 128))
    kern = functools.partial(_inproj_kernel, ctx_len=ctx_len, tm=tm)
    return pl.pallas_call(
        kern,
        grid=(b, s // tm, n // tn),
        in_specs=[pl.BlockSpec((1, tm, d), lambda i, t, j: (i, t, 0)),
                  pl.BlockSpec((1, 1, d), lambda i, t, j: (i, 0, 0)),
                  pl.BlockSpec((1, 1, d), lambda i, t, j: (i, 0, 0)),
                  pl.BlockSpec((1, d), lambda i, t, j: (0, 0)),
                  pl.BlockSpec((1, d), lambda i, t, j: (0, 0)),
                  pl.BlockSpec((d, tn), lambda i, t, j: (0, j))],
        out_specs=pl.BlockSpec((1, tm, tn), lambda i, t, j: (i, t, j)),
        out_shape=jax.ShapeDtypeStruct((b, s, n), BF16),
        scratch_shapes=[pltpu.VMEM((tm, d), BF16)],
        compiler_params=pltpu.CompilerParams(
            dimension_semantics=("parallel", "parallel", "arbitrary"),
            vmem_limit_bytes=VMEM_LIMIT),
        name="in_projection",
    )(h, sh_l, sc_l, sh_c, sc_c, w_p)


def _ret_kernel(lg_ref, qk_ref, v_ref, g_ref, cos_ref, sin_ref, o_ref,
                p_ref, q2_ref, ds_ref, sp_ref, *, ctx_len, seq, c):
    n_all, n_ctx = seq // c, ctx_len // c
    dk = RET_DK
    lane = lax.broadcasted_iota(jnp.int32, (c, LANES), 1)
    lg = lg_ref[0]
    pos = lax.broadcasted_iota(jnp.int32, (c, LANES), 0).astype(F32)
    is_q = lane < dk
    fac_f = jnp.exp(jnp.where(is_q, pos + 1.0, c - 1.0 - pos) * lg)
    fac_b = jnp.exp(jnp.where(is_q, c - pos, pos) * lg)
    rr = lax.broadcasted_iota(jnp.int32, (c, c), 0)
    cc = lax.broadcasted_iota(jnp.int32, (c, c), 1)
    dist = jnp.abs(rr - cc).astype(F32)
    dmat = jnp.exp(dist * lg[:, :1]) * jnp.where(rr == cc, 2.0, 1.0)
    gc = jnp.exp(lg * float(c))

    per_step = _pick_tile(n_all, (3, 1))

    def over_chunks(fn):
        def body(gi, carry):
            for j in range(per_step):
                fn(gi * per_step + j)
            return carry
        lax.fori_loop(0, n_all // per_step, body, 0)

    def local(ci):
        rows = pl.ds(pl.multiple_of(ci * c, c), c)
        x = qk_ref[0, rows, :].astype(F32)
        swapped = jnp.where(lane % 32 < 16, pltpu.roll(x, LANES - 16, 1), pltpu.roll(x, 16, 1))
        qk = x * cos_ref[rows, :] + swapped * sin_ref[rows, :]
        p_ref[ci] = (_dot_nt(qk[:, :dk].astype(BF16), qk[:, dk:].astype(BF16)) * dmat).astype(BF16)
        qk_f = qk * fac_f
        qk_b = qk * fac_b
        q2_ref[rows, :] = jnp.where(is_q, qk_f, pltpu.roll(qk_b, dk, 1)).astype(BF16)
        k2 = jnp.where(is_q, pltpu.roll(qk_f, dk, 1), qk_b).astype(BF16)
        ds_ref[ci] = _dot_tn(k2, v_ref[0, rows, :])

    over_chunks(local)

    def step(it, carry):
        s_f, s_b = carry
        cf, cb = _chunk_order(it, n_ctx, n_all)
        sp_ref[cf, 0:dk, :] = s_f.astype(BF16)
        sp_ref[cb, dk:2 * dk, :] = s_b.astype(BF16)
        return gc * s_f + ds_ref[cf, 0:dk, :], gc * s_b + ds_ref[cb, dk:2 * dk, :]

    zero = jnp.zeros((dk, RET_DV), F32)
    lax.fori_loop(0, n_all, step, (zero, zero))

    def readout(ci):
        rows = pl.ds(pl.multiple_of(ci * c, c), c)
        lhs = jnp.concatenate([p_ref[ci], q2_ref[rows, :]], axis=1)
        rhs = jnp.concatenate([v_ref[0, rows, :], sp_ref[ci]], axis=0)
        o = _ln0(_dot(lhs, rhs))
        o_ref[0, rows, :] = (o * _silu(g_ref[0, rows, :].astype(F32))).astype(o_ref.dtype)

    over_chunks(readout)


def _retention(z, lg_rows, cos_t, sin_t, ctx_len):
    b, s, _ = z.shape
    c = _pick_tile(math.gcd(ctx_len, s), (RET_CHUNK, 128))
    n = s // c
    kern = functools.partial(_ret_kernel, ctx_len=ctx_len, seq=s, c=c)

    def col(blk):
        return pl.BlockSpec((1, s, LANES), lambda i, h: (i, 0, blk + h))

    return pl.pallas_call(
        kern,
        grid=(b, RET_HEADS),
        in_specs=[pl.BlockSpec((1, 1, LANES), lambda i, h: (h, 0, 0)),
                  col(BLK_RET_QK), col(BLK_RET_V), col(BLK_RET_G),
                  pl.BlockSpec((s, LANES), lambda i, h: (0, 0)),
                  pl.BlockSpec((s, LANES), lambda i, h: (0, 0))],
        out_specs=pl.BlockSpec((1, s, LANES), lambda i, h: (i, 0, h)),
        out_shape=jax.ShapeDtypeStruct((b, s, RET_HEADS * RET_DV), BF16),
        scratch_shapes=[pltpu.VMEM((n, c, c), BF16),
                        pltpu.VMEM((s, LANES), BF16),
                        pltpu.VMEM((n, 2 * RET_DK, RET_DV), F32),
                        pltpu.VMEM((n, 2 * RET_DK, RET_DV), BF16)],
        compiler_params=pltpu.CompilerParams(
            dimension_semantics=("parallel", "parallel"), vmem_limit_bytes=VMEM_LIMIT),
        name="retention",
    )(lg_rows, z, z, z, cos_t, sin_t)


def _hg_kernel(q_ref, ff_ref, fb_ref, i_ref, g_ref, lbf_ref, lbb_ref, nw_ref, a3_ref, m_ref, o_ref,
               acc_ref, q2_ref, ds_ref, sp_ref, dec_ref, *, ctx_len, seq):
    c = GLA_CHUNK
    n_all, n_ctx = seq // c, ctx_len // c
    nl = m_ref.shape[0] - 1
    dk = HG_DK

    def stack4(x_f, x_b):
        return jnp.concatenate([x_f[:c], x_b[:c], x_f[c:], x_b[c:]], axis=0)

    def local(gi, carry):
        rows2 = pl.ds(pl.multiple_of(gi * 2 * c, 2 * c), 2 * c)
        q = _silu(q_ref[0, rows2, :].astype(F32))
        v = i_ref[0, rows2, :]
        kk, r = [], []
        for d, (f_ref, lb_ref) in enumerate(((ff_ref, lbf_ref), (fb_ref, lbb_ref))):
            lb = lb_ref[...]
            f = lb + (1.0 - lb) * jax.nn.sigmoid(f_ref[0, rows2, :].astype(F32))
            g = jnp.log(f)
            kk.append(1.0 - f)
            r.append([_dot(a3_ref[d], jnp.concatenate(_split3(g[j * c:(j + 1) * c]), axis=0))
                      for j in range(2)])

        def rows_of(lo, n=c):
            return [r[d][j][lo:lo + n] for j in range(2) for d in range(2)]

        q4 = stack4(q, q)
        k4 = stack4(kk[0], kk[1])
        scores = m_ref[nl] * _dot_nt(q4.astype(BF16), k4.astype(BF16))
        for l in range(nl):
            e = jnp.exp(jnp.concatenate(rows_of(l * c), axis=0))
            scores = scores + m_ref[l] * _dot_nt((q4 * e).astype(BF16), (k4 * e).astype(BF16))
        v4 = jnp.concatenate([v[:c], v[:c], v[c:], v[c:]], axis=0)
        o4 = _dot(scores.astype(BF16), v4)
        acc_ref[rows2, :] = jnp.concatenate([o4[:c] + o4[c:2 * c], o4[2 * c:3 * c] + o4[3 * c:]], axis=0)

        ecum = [jnp.exp(x) for x in rows_of(nl * c)]
        erest = [jnp.exp(x) for x in rows_of((nl + 1) * c)]
        etot = [jnp.exp(x) for x in rows_of((nl + 2) * c, 8)]
        for j in range(2):
            ci = 2 * gi + j
            qj = q[j * c:(j + 1) * c]
            rows = pl.ds(pl.multiple_of(ci * c, c), c)
            q2_ref[rows, :] = jnp.concatenate([qj * ecum[2 * j], qj * ecum[2 * j + 1]], axis=1).astype(BF16)
            k2 = jnp.concatenate([kk[0][j * c:(j + 1) * c] * erest[2 * j],
                                  kk[1][j * c:(j + 1) * c] * erest[2 * j + 1]], axis=1).astype(BF16)
            ds_ref[ci] = _dot_tn(v[j * c:(j + 1) * c], k2)
            dec_ref[ci] = jnp.concatenate([etot[2 * j], etot[2 * j + 1]], axis=1)
        return carry

    lax.fori_loop(0, n_all // 2, local, 0)

    def step(it, carry):
        st_f, st_b = carry
        cf, cb = _chunk_order(it, n_ctx, n_all)
        sp_ref[cf, :, 0:dk] = st_f.astype(BF16)
        sp_ref[cb, :, dk:2 * dk] = st_b.astype(BF16)
        st_f = st_f * dec_ref[cf][0:1, 0:dk] + ds_ref[cf, :, 0:dk]
        st_b = st_b * dec_ref[cb][0:1, dk:2 * dk] + ds_ref[cb, :, dk:2 * dk]
        return st_f, st_b

    zero = jnp.zeros((HG_DV, dk), F32)
    lax.fori_loop(0, n_all, step, (zero, zero))

    per_step = _pick_tile(n_all, (4, 2))

    def readout(gi, carry):
        for j in range(per_step):
            ci = per_step * gi + j
            rows = pl.ds(pl.multiple_of(ci * c, c), c)
            o = acc_ref[rows, :] + _dot_nt(q2_ref[rows, :], sp_ref[ci])
            o = o * lax.rsqrt(jnp.mean(o * o, axis=-1, keepdims=True) + NORM_EPS)
            o_ref[0, rows, :] = (o * nw_ref[...] * _silu(g_ref[0, rows, :].astype(F32))).astype(o_ref.dtype)
        return carry

    lax.fori_loop(0, n_all // per_step, readout, 0)


def _hgrn2(z, lb_f, lb_b, norm_w, ctx_len):
    b, s, _ = z.shape
    c = GLA_CHUNK
    n = s // c
    assert n % 2 == 0
    a3, masks = _hg_consts(c)
    a3 = jnp.asarray(a3, BF16)
    masks = jnp.asarray(masks, F32)
    kern = functools.partial(_hg_kernel, ctx_len=ctx_len, seq=s)

    def col(blk):
        return pl.BlockSpec((1, s, LANES), lambda i, h: (i, 0, blk + h))

    def full(a):
        nd = a.ndim
        return pl.BlockSpec(a.shape, lambda i, h: (0,) * nd)

    return pl.pallas_call(
        kern,
        grid=(b, HG_HEADS),
        in_specs=[col(BLK_HG_Q), col(BLK_HG_FF), col(BLK_HG_FB), col(BLK_HG_I), col(BLK_HG_G),
                  pl.BlockSpec((1, LANES), lambda i, h: (0, h)),
                  pl.BlockSpec((1, LANES), lambda i, h: (0, h)),
                  pl.BlockSpec((1, LANES), lambda i, h: (0, 0)),
                  full(a3), full(masks)],
        out_specs=pl.BlockSpec((1, s, LANES), lambda i, h: (i, 0, h)),
        out_shape=jax.ShapeDtypeStruct((b, s, HG_HEADS * HG_DV), BF16),
        scratch_shapes=[pltpu.VMEM((s, HG_DV), F32),
                        pltpu.VMEM((s, 2 * HG_DK), BF16),
                        pltpu.VMEM((n, HG_DV, 2 * HG_DK), F32),
                        pltpu.VMEM((n, HG_DV, 2 * HG_DK), BF16),
                        pltpu.VMEM((n, 8, 2 * HG_DK), F32)],
        compiler_params=pltpu.CompilerParams(
            dimension_semantics=("parallel", "parallel"), vmem_limit_bytes=VMEM_LIMIT),
        name="hgrn2",
    )(z, z, z, z, z, lb_f, lb_b, norm_w, a3, masks)


def _conv_rows(ref, r0, c, seq, ctx_len, w, taps_left, cols=slice(None)):
    lo = pl.multiple_of(jnp.maximum(r0 - HALO, 0), HALO)
    hi = pl.multiple_of(jnp.minimum(r0 + c, seq - HALO), HALO)
    xh = jnp.concatenate([ref[0, pl.ds(lo, HALO), cols].astype(F32),
                          ref[0, pl.ds(pl.multiple_of(r0, HALO), c), cols].astype(F32),
                          ref[0, pl.ds(hi, HALO), cols].astype(F32)], axis=0)
    rows = _row_ids((c + 2 * HALO, 1), r0 - HALO)
    seg_lo = jnp.where(r0 < ctx_len, 0, ctx_len)
    seg_hi = jnp.where(r0 < ctx_len, ctx_len, seq)
    xh = jnp.where((rows >= seg_lo) & (rows < seg_hi), xh, 0.0)
    out = None
    for j in range(w.shape[0]):
        start = HALO + j - taps_left
        term = xh[start:start + c] * w[j:j + 1]
        out = term if out is None else out + term
    return out


def _gdn_kernel(qk_ref, v_ref, g_ref, misc_ref, cw_qk_ref, cw_v_ref, par_ref, nw_ref,
                a3_ref, lv_ref, incl_ref, o_ref,
                qkn_ref, va_ref, acc_ref, lhs1_ref, lhs2_ref, u0_ref, el_ref,
                base_ref, xs_ref, t_ref, rhs_ref, *, ctx_len, seq):
    c = GLA_CHUNK
    cq = 4 * c
    n_all, n_ctx = seq // c, ctx_len // c
    h = pl.program_id(1)
    eye = (lax.broadcasted_iota(jnp.int32, (cq, cq), 0)
           == lax.broadcasted_iota(jnp.int32, (cq, cq), 1)).astype(F32)

    cp = 2 * c if ctx_len % (2 * c) == 0 else c
    lane = lax.broadcasted_iota(jnp.int32, (cp, LANES), 1)
    is_q = lane < GDN_DK

    def prep(ci, carry):
        r0 = ci * cp
        rows = pl.ds(pl.multiple_of(r0, cp), cp)
        qk = _silu(_conv_rows(qk_ref, r0, cp, seq, ctx_len, cw_qk_ref[...], GDN_CONV // 2))
        va = _silu(_conv_rows(v_ref, r0, cp, seq, ctx_len, cw_v_ref[...], GDN_CONV // 2))
        sq = qk * qk
        s_q = jnp.sum(jnp.where(is_q, sq, 0.0), axis=-1, keepdims=True)
        s_k = jnp.sum(jnp.where(is_q, 0.0, sq), axis=-1, keepdims=True)
        inv = lax.rsqrt(jnp.where(is_q, s_q, s_k) + NORM_EPS)
        qkn_ref[rows, :] = qk * inv * jnp.where(is_q, GDN_DK ** -0.5, 1.0)
        va_ref[rows, :] = va
        return carry

    lax.fori_loop(0, seq // cp, prep, 0)

    def stack4(x_f, x_b):
        return jnp.concatenate([x_f[:c], x_b[:c], x_f[c:], x_b[c:]], axis=0)

    rr = lax.broadcasted_iota(jnp.int32, (LANES, 4 * LANES), 0)
    cc4 = lax.broadcasted_iota(jnp.int32, (LANES, 4 * LANES), 1) // LANES
    pick = (rr == cc4 * GDN_HEADS + h).astype(BF16)
    lane = lax.broadcasted_iota(jnp.int32, (cq, LANES), 1)
    low = lane < GDN_DK
    rowb = lax.broadcasted_iota(jnp.int32, (cq, LANES), 0) // c
    own_half = (rowb % 2 == 0) == low
    e_hi = (lane == 0).astype(BF16)
    eye_k = (lax.broadcasted_iota(jnp.int32, (GDN_DK, LANES), 0)
             == lax.broadcasted_iota(jnp.int32, (GDN_DK, LANES), 1)).astype(BF16)
    half_t = [lax.broadcasted_iota(jnp.int32, (GDN_DK, LANES), 1) < GDN_DK,
              lax.broadcasted_iota(jnp.int32, (GDN_DK, LANES), 1) >= GDN_DK]

    def local_quad(gi):
        rows2 = pl.ds(pl.multiple_of(gi * 2 * c, 2 * c), 2 * c)
        qkn = qkn_ref[rows2, :]
        va = va_ref[rows2, :]
        ab = _dot(misc_ref[0, rows2, :], pick)
        loga, beta = [], []
        for d in range(2):
            a_log = par_ref[0, 2 * d:2 * d + 1, :]
            dt_b = par_ref[0, 2 * d + 1:2 * d + 2, :]
            loga.append(-jnp.exp(a_log) * jax.nn.softplus(ab[:, d * LANES:(d + 1) * LANES] + dt_b))
            beta.append(jax.nn.sigmoid(ab[:, (2 + d) * LANES:(3 + d) * LANES]))
        loga = stack4(loga[0], loga[1])
        beta = stack4(beta[0], beta[1])
        qk4 = stack4(qkn, qkn)
        kq4 = pltpu.roll(qk4, GDN_DK, 1)
        q2 = jnp.where(low, qk4, kq4)
        k2 = jnp.where(low, kq4, qk4)
        v4 = stack4(va, va)

        hi, mid, _ = _split3(loga)
        r = _dot(a3_ref[...], jnp.concatenate([hi, mid], axis=0))
        cum, total = r[:cq], r[cq:]
        chi, cmid, _ = _split3(cum)
        cum_row = _dot_nt(jnp.concatenate([e_hi, e_hi], axis=1),
                          jnp.concatenate([chi, cmid], axis=1))
        dec = jnp.exp(jnp.minimum(jnp.concatenate([cum, cum], axis=1) - cum_row, 0.0))
        k_one = jnp.where(low, k2, 0.0).astype(BF16)
        kq = _dot_nt(jnp.concatenate([k2.astype(BF16), q2.astype(BF16)], axis=0), k_one)
        base = (jnp.concatenate([beta, beta], axis=1) * dec) * kq[:cq]
        base_ref[gi] = base
        xs_ref[gi] = eye - base * lv_ref[0]
        ecum = jnp.exp(cum)
        rhs_ref[gi] = jnp.concatenate([beta * ecum * k2, beta * v4], axis=1).astype(BF16)
        qkd = (kq[cq:] * (incl_ref[...] * dec)).astype(BF16)
        qt = jnp.where(own_half, q2 * ecum, 0.0).astype(BF16)
        kt = (k2 * jnp.exp(total - cum)).astype(BF16)
        ktt = _dot_nt(eye_k, kt)
        etot = jnp.exp(total)
        for j in range(2):
            for d in range(2):
                lo_r = (2 * j + d) * c
                st = step_of(2 * gi + j, d)
                lhs1_ref[st, (2 + d) * c:(3 + d) * c, :] = qt[lo_r:lo_r + c]
                lhs2_ref[st, d * c:(d + 1) * c, :] = qkd[lo_r:lo_r + c, j * LANES:(j + 1) * LANES]
                lhs2_ref[st, (2 + d) * c:(3 + d) * c, :] = jnp.where(
                    half_t[d], ktt[:, j * LANES:(j + 1) * LANES], 0.0).astype(BF16)
                el_ref[d, st] = etot[lo_r:lo_r + 8]

    def step_of(ci, d):
        return ci if d == 0 else jnp.where(ci < n_ctx, n_ctx - 1 - ci, n_all + n_ctx - 1 - ci)

    nq = n_all // 2

    def over_stacks(fn, sizes):
        group = _pick_tile(nq, sizes)

        def body(gj, carry):
            for k in range(group):
                fn(gj * group + k)
            return carry
        lax.fori_loop(0, nq // group, body, 0)

    over_stacks(local_quad, (GDN_GROUP, 1))

    for l in range(1, lv_ref.shape[0]):
        def left(gi, l=l):
            nm = (base_ref[gi] * lv_ref[l]).astype(BF16)
            t_ref[gi] = _dot(xs_ref[gi].astype(BF16), nm).astype(BF16)

        def right(gi):
            x = xs_ref[gi]
            xs_ref[gi] = x - _dot(t_ref[gi], x.astype(BF16))

        over_stacks(left, (2 * GDN_GROUP, GDN_GROUP, 1))
        over_stacks(right, (2 * GDN_GROUP, GDN_GROUP, 1))

    def finish(gi):
        wu = _dot(xs_ref[gi].astype(BF16), rhs_ref[gi])
        w = jnp.where(own_half, wu[:, :LANES], 0.0).astype(BF16)
        for j in range(2):
            for d in range(2):
                lo_r = (2 * j + d) * c
                st = step_of(2 * gi + j, d)
                lhs1_ref[st, d * c:(d + 1) * c, :] = w[lo_r:lo_r + c]
                u0_ref[st, d * c:(d + 1) * c, :] = wu[lo_r:lo_r + c, LANES:]

    over_stacks(finish, (GDN_GROUP, 1))

    is_fwd_row = lax.broadcasted_iota(jnp.int32, (2 * GDN_DK, LANES), 0) < GDN_DK

    def step(it, s2):
        cf, cb = _chunk_order(it, n_ctx, n_all)
        r1 = _dot(lhs1_ref[it], s2.astype(BF16))
        u = u0_ref[it] - r1[:2 * c]
        r2 = _dot(lhs2_ref[it], u.astype(BF16))
        o2 = r1[2 * c:] + r2[:2 * c]
        acc_ref[0, pl.ds(pl.multiple_of(cf * c, c), c), :] = o2[:c]
        acc_ref[1, pl.ds(pl.multiple_of(cb * c, c), c), :] = o2[c:]
        el2 = jnp.where(is_fwd_row, el_ref[0, it][0:1], el_ref[1, it][0:1])
        return el2 * s2 + r2[2 * c:]

    lax.fori_loop(0, n_all, step, jnp.zeros((2 * GDN_DK, GDN_DV), F32))

    def readout(ci, carry):
        rows = pl.ds(pl.multiple_of(ci * cp, cp), cp)
        o = acc_ref[0, rows, :] + acc_ref[1, rows, :]
        o = o * lax.rsqrt(jnp.mean(o * o, axis=-1, keepdims=True) + NORM_EPS)
        o_ref[0, rows, :] = (o * nw_ref[...] * _silu(g_ref[0, rows, :].astype(F32))).astype(o_ref.dtype)
        return carry

    lax.fori_loop(0, seq // cp, readout, 0)


GDN_PAIR = 2


def _gdn_pair_kernel(qk_ref, v_ref, g_ref, misc_ref, cw_qk_ref, cw_v_ref, par_ref, nw_ref,
                     a3_ref, lv_ref, incl_ref, o_ref,
                     qkn_ref, va_ref, acc_ref, lhs1_ref, lhs2_ref, u0_ref, el_ref,
                     base_ref, xs_ref, t_ref, rhs_ref, *, ctx_len, seq):
    c = GLA_CHUNK
    cq = 4 * c
    n_all, n_ctx = seq // c, ctx_len // c
    nq = n_all // 2
    eye = (lax.broadcasted_iota(jnp.int32, (cq, cq), 0)
           == lax.broadcasted_iota(jnp.int32, (cq, cq), 1)).astype(F32)
    cp = 2 * c if ctx_len % (2 * c) == 0 else c
    lane_p = lax.broadcasted_iota(jnp.int32, (cp, LANES), 1)
    is_q = lane_p < GDN_DK

    rr = lax.broadcasted_iota(jnp.int32, (LANES, 4 * LANES), 0)
    cc4 = lax.broadcasted_iota(jnp.int32, (LANES, 4 * LANES), 1) // LANES
    lane = lax.broadcasted_iota(jnp.int32, (cq, LANES), 1)
    low = lane < GDN_DK
    rowb = lax.broadcasted_iota(jnp.int32, (cq, LANES), 0) // c
    own_half = (rowb % 2 == 0) == low
    e_hi = (lane == 0).astype(BF16)
    lane_k = lax.broadcasted_iota(jnp.int32, (GDN_DK, LANES), 1)
    eye_k = (lax.broadcasted_iota(jnp.int32, (GDN_DK, LANES), 0) == lane_k).astype(BF16)
    half_t = [lane_k < GDN_DK, lane_k >= GDN_DK]

    def stack4(x_f, x_b):
        return jnp.concatenate([x_f[:c], x_b[:c], x_f[c:], x_b[c:]], axis=0)

    def step_of(ci, d):
        return ci if d == 0 else jnp.where(ci < n_ctx, n_ctx - 1 - ci, n_all + n_ctx - 1 - ci)

    def over_stacks(fn, sizes):
        group = _pick_tile(nq, sizes)

        def body(gj, carry):
            for k in range(group):
                fn(gj * group + k)
            return carry
        lax.fori_loop(0, nq // group, body, 0)

    def chunk_local(hh):
        h = pl.program_id(1) * GDN_PAIR + hh
        cols = slice(hh * LANES, (hh + 1) * LANES)
        pick = (rr == cc4 * GDN_HEADS + h).astype(BF16)

        def prep(ci, carry):
            r0 = ci * cp
            rows = pl.ds(pl.multiple_of(r0, cp), cp)
            qk = _silu(_conv_rows(qk_ref, r0, cp, seq, ctx_len, cw_qk_ref[:, cols], GDN_CONV // 2, cols))
            va = _silu(_conv_rows(v_ref, r0, cp, seq, ctx_len, cw_v_ref[:, cols], GDN_CONV // 2, cols))
            sq = qk * qk
            s_q = jnp.sum(jnp.where(is_q, sq, 0.0), axis=-1, keepdims=True)
            s_k = jnp.sum(jnp.where(is_q, 0.0, sq), axis=-1, keepdims=True)
            inv = lax.rsqrt(jnp.where(is_q, s_q, s_k) + NORM_EPS)
            qkn_ref[rows, :] = qk * inv * jnp.where(is_q, GDN_DK ** -0.5, 1.0)
            va_ref[rows, :] = va
            return carry

        lax.fori_loop(0, seq // cp, prep, 0)

        def setup(gi):
            rows2 = pl.ds(pl.multiple_of(gi * 2 * c, 2 * c), 2 * c)
            qkn = qkn_ref[rows2, :]
            va = va_ref[rows2, :]
            ab = _dot(misc_ref[0, rows2, :], pick)
            loga, beta = [], []
            for d in range(2):
                a_log = par_ref[hh, 2 * d:2 * d + 1, :]
                dt_b = par_ref[hh, 2 * d + 1:2 * d + 2, :]
                loga.append(-jnp.exp(a_log) * jax.nn.softplus(ab[:, d * LANES:(d + 1) * LANES] + dt_b))
                beta.append(jax.nn.sigmoid(ab[:, (2 + d) * LANES:(3 + d) * LANES]))
            loga = stack4(loga[0], loga[1])
            beta = stack4(beta[0], beta[1])
            qk4 = stack4(qkn, qkn)
            kq4 = pltpu.roll(qk4, GDN_DK, 1)
            q2 = jnp.where(low, qk4, kq4)
            k2 = jnp.where(low, kq4, qk4)
            v4 = stack4(va, va)

            hi, mid, _ = _split3(loga)
            r = _dot(a3_ref[...], jnp.concatenate([hi, mid], axis=0))
            cum, total = r[:cq], r[cq:]
            chi, cmid, _ = _split3(cum)
            cum_row = _dot_nt(jnp.concatenate([e_hi, e_hi], axis=1),
                              jnp.concatenate([chi, cmid], axis=1))
            dec = jnp.exp(jnp.minimum(jnp.concatenate([cum, cum], axis=1) - cum_row, 0.0))
            k_one = jnp.where(low, k2, 0.0).astype(BF16)
            kq = _dot_nt(jnp.concatenate([k2.astype(BF16), q2.astype(BF16)], axis=0), k_one)
            base = (jnp.concatenate([beta, beta], axis=1) * dec) * kq[:cq]
            base_ref[gi] = base.astype(BF16)
            xs_ref[gi] = eye - base * lv_ref[0].astype(F32)
            ecum = jnp.exp(cum)
            rhs_ref[gi] = jnp.concatenate([beta * ecum * k2, beta * v4], axis=1).astype(BF16)
            qkd = (kq[cq:] * (incl_ref[...] * dec)).astype(BF16)
            qt = jnp.where(own_half, q2 * ecum, 0.0).astype(BF16)
            kt = (k2 * jnp.exp(total - cum)).astype(BF16)
            ktt = _dot_nt(eye_k, kt)
            etot = jnp.exp(total)
            for j in range(2):
                for d in range(2):
                    lo_r = (2 * j + d) * c
                    st = step_of(2 * gi + j, d)
                    lhs1_ref[hh, st, (2 + d) * c:(3 + d) * c, :] = qt[lo_r:lo_r + c]
                    lhs2_ref[hh, st, d * c:(d + 1) * c, :] = qkd[lo_r:lo_r + c, j * LANES:(j + 1) * LANES]
                    lhs2_ref[hh, st, (2 + d) * c:(3 + d) * c, :] = jnp.where(
                        half_t[d], ktt[:, j * LANES:(j + 1) * LANES], 0.0).astype(BF16)
                    el_ref[hh, d, st] = etot[lo_r:lo_r + 8]

        over_stacks(setup, (GDN_GROUP, 1))

        for l in range(1, lv_ref.shape[0]):
            def left(gi, l=l):
                t_ref[gi] = _dot(xs_ref[gi].astype(BF16), base_ref[gi] * lv_ref[l]).astype(BF16)

            def right(gi):
                x = xs_ref[gi]
                xs_ref[gi] = x - _dot(t_ref[gi], x.astype(BF16))

            over_stacks(left, (2 * GDN_GROUP, GDN_GROUP, 1))
            over_stacks(right, (2 * GDN_GROUP, GDN_GROUP, 1))

        def finish(gi):
            wu = _dot(xs_ref[gi].astype(BF16), rhs_ref[gi])
            w = jnp.where(own_half, wu[:, :LANES], 0.0).astype(BF16)
            for j in range(2):
                for d in range(2):
                    lo_r = (2 * j + d) * c
                    st = step_of(2 * gi + j, d)
                    lhs1_ref[hh, st, d * c:(d + 1) * c, :] = w[lo_r:lo_r + c]
                    u0_ref[hh, st, d * c:(d + 1) * c, :] = wu[lo_r:lo_r + c, LANES:]

        over_stacks(finish, (GDN_GROUP, 1))

    for hh in range(GDN_PAIR):
        chunk_local(hh)

    is_fwd_row = lax.broadcasted_iota(jnp.int32, (2 * GDN_DK, LANES), 0) < GDN_DK

    def step(it, states):
        cf, cb = _chunk_order(it, n_ctx, n_all)
        rows_f = pl.ds(pl.multiple_of(cf * c, c), c)
        rows_b = pl.ds(pl.multiple_of(cb * c, c), c)
        r1 = [_dot(lhs1_ref[hh, it], states[hh].astype(BF16)) for hh in range(GDN_PAIR)]
        u = [u0_ref[hh, it] - r1[hh][:2 * c] for hh in range(GDN_PAIR)]
        r2 = [_dot(lhs2_ref[hh, it], u[hh].astype(BF16)) for hh in range(GDN_PAIR)]
        new = []
        for hh in range(GDN_PAIR):
            o2 = r1[hh][2 * c:] + r2[hh][:2 * c]
            acc_ref[hh, 0, rows_f, :] = o2[:c]
            acc_ref[hh, 1, rows_b, :] = o2[c:]
            el2 = jnp.where(is_fwd_row, el_ref[hh, 0, it][0:1], el_ref[hh, 1, it][0:1])
            new.append(el2 * states[hh] + r2[hh][2 * c:])
        return tuple(new)

    zero = jnp.zeros((2 * GDN_DK, GDN_DV), F32)
    lax.fori_loop(0, n_all, step, (zero,) * GDN_PAIR)

    def readout(ci, carry):
        rows = pl.ds(pl.multiple_of(ci * cp, cp), cp)
        for hh in range(GDN_PAIR):
            cols = slice(hh * LANES, (hh + 1) * LANES)
            o = acc_ref[hh, 0, rows, :] + acc_ref[hh, 1, rows, :]
            o = o * lax.rsqrt(jnp.mean(o * o, axis=-1, keepdims=True) + NORM_EPS)
            o_ref[0, rows, cols] = (o * nw_ref[...] * _silu(g_ref[0, rows, cols].astype(F32))).astype(o_ref.dtype)
        return carry

    lax.fori_loop(0, seq // cp, readout, 0)


def _gdn(z, conv_w, par, norm_w, ctx_len):
    b, s, _ = z.shape
    c = GLA_CHUNK
    n = s // c
    assert n % 2 == 0
    a3, lv, incl = _gdn_consts(c)
    a3 = jnp.asarray(np.tile(a3, (1, 2)), BF16)
    lv = jnp.asarray(lv, BF16)
    incl = jnp.asarray(incl, F32)
    kern = functools.partial(_gdn_pair_kernel, ctx_len=ctx_len, seq=s)
    hp = GDN_PAIR
    wide = hp * LANES

    def col(blk):
        return pl.BlockSpec((1, s, wide), lambda i, p: (i, 0, blk // hp + p))

    def full(a):
        nd = a.ndim
        return pl.BlockSpec(a.shape, lambda i, p: (0,) * nd)

    return pl.pallas_call(
        kern,
        grid=(b, GDN_HEADS // hp),
        in_specs=[col(BLK_GDN_QK), col(BLK_GDN_V), col(BLK_GDN_G),
                  pl.BlockSpec((1, s, LANES), lambda i, p: (i, 0, BLK_GDN_MISC)),
                  pl.BlockSpec((GDN_CONV, wide), lambda i, p: (0, p)),
                  pl.BlockSpec((GDN_CONV, wide), lambda i, p: (0, GDN_HEADS // hp + p)),
                  pl.BlockSpec((hp, 8, LANES), lambda i, p: (p, 0, 0)),
                  pl.BlockSpec((1, LANES), lambda i, p: (0, 0)),
                  full(a3), full(lv), full(incl)],
        out_specs=pl.BlockSpec((1, s, wide), lambda i, p: (i, 0, p)),
        out_shape=jax.ShapeDtypeStruct((b, s, GDN_HEADS * GDN_DV), BF16),
        scratch_shapes=[pltpu.VMEM((s, LANES), F32),
                        pltpu.VMEM((s, LANES), F32),
                        pltpu.VMEM((hp, 2, s, LANES), F32),
                        pltpu.VMEM((hp, n, 4 * c, 2 * GDN_DK), BF16),
                        pltpu.VMEM((hp, n, 4 * c, 2 * c), BF16),
                        pltpu.VMEM((hp, n, 2 * c, GDN_DV), F32),
                        pltpu.VMEM((hp, 2, n, 8, LANES), F32),
                        pltpu.VMEM((n // 2, 4 * c, 4 * c), BF16),
                        pltpu.VMEM((n // 2, 4 * c, 4 * c), F32),
                        pltpu.VMEM((n // 2, 4 * c, 4 * c), BF16),
                        pltpu.VMEM((n // 2, 4 * c, LANES + GDN_DV), BF16)],
        compiler_params=pltpu.CompilerParams(
            dimension_semantics=("parallel", "parallel"), vmem_limit_bytes=VMEM_LIMIT),
        name="gated_deltanet",
    )(z, z, z, z, conv_w, conv_w, par, norm_w, a3, lv, incl)


def _lru_kernel(x_ref, gate_ref, cw_ref, cb_ref, wd_ref, bd_ref, lam_ref, o_ref,
                af_ref, uf_ref, ab_ref, ub_ref, hf_ref, hb_ref, *, ctx_len, seq, c):
    w = LRU_WIDTH
    n_all, n_ctx = seq // c, ctx_len // c

    def gates(ci, d, a_ref, u_ref):
        r0 = ci * c
        xc = _conv_rows(x_ref, r0, c, seq, ctx_len, cw_ref[...], LRU_CONV // 2) + cb_ref[...]
        y = _dot(xc.astype(BF16), wd_ref[d]) + bd_ref[d]
        r = jax.nn.sigmoid(y[:, :w])
        i = jax.nn.sigmoid(y[:, w:])
        log_a = -LRU_C * r * jax.nn.softplus(-lam_ref[d])
        a = jnp.exp(log_a)
        a_ref[...] = a
        u_ref[...] = jnp.sqrt(1.0 - a * a) * (i * xc)

    def step(it, carry):
        h_f, h_b = carry
        cf, cb = _chunk_order(it, n_ctx, n_all)
        gates(cf, 0, af_ref, uf_ref)
        gates(cb, 1, ab_ref, ub_ref)
        base_f = cf * c
        base_b = cb * c

        def row(t, hh):
            hf, hb = hh
            tb = c - 1 - t
            hf = af_ref[pl.ds(t, 1), :] * hf + uf_ref[pl.ds(t, 1), :]
            hb = ab_ref[pl.ds(tb, 1), :] * hb + ub_ref[pl.ds(tb, 1), :]
            hf_ref[pl.ds(base_f + t, 1), :] = hf
            hb_ref[pl.ds(base_b + tb, 1), :] = hb
            return hf, hb

        return lax.fori_loop(0, c, row, (h_f, h_b), unroll=8)

    zero = jnp.zeros((1, w), F32)
    lax.fori_loop(0, n_all, step, (zero, zero))

    def readout(ci, carry):
        rows = pl.ds(pl.multiple_of(ci * c, c), c)
        hsum = hf_ref[rows, :] + hb_ref[rows, :]
        o_ref[0, rows, :] = (jax.nn.gelu(gate_ref[0, rows, :].astype(F32)) * hsum).astype(o_ref.dtype)
        return carry

    lax.fori_loop(0, n_all, readout, 0)


def _rglru(z, conv_w, conv_b, w_dense, b_dense, lam, ctx_len):
    b, s, _ = z.shape
    w = LRU_WIDTH
    c = _pick_tile(math.gcd(ctx_len, s), (LRU_ROWS, 128, 64))
    kern = functools.partial(_lru_kernel, ctx_len=ctx_len, seq=s, c=c)

    def full(a):
        nd = a.ndim
        return pl.BlockSpec(a.shape, lambda i: (0,) * nd)

    return pl.pallas_call(
        kern,
        grid=(b,),
        in_specs=[pl.BlockSpec((1, s, w), lambda i: (i, 0, BLK_LRU_X * LANES // w)),
                  pl.BlockSpec((1, s, w), lambda i: (i, 0, BLK_LRU_G * LANES // w)),
                  full(conv_w), full(conv_b), full(w_dense), full(b_dense), full(lam)],
        out_specs=pl.BlockSpec((1, s, w), lambda i: (i, 0, 0)),
        out_shape=jax.ShapeDtypeStruct((b, s, w), BF16),
        scratch_shapes=[pltpu.VMEM((c, w), F32)] * 4 + [pltpu.VMEM((s, w), F32)] * 2,
        compiler_params=pltpu.CompilerParams(
            dimension_semantics=("parallel",), vmem_limit_bytes=VMEM_LIMIT),
        name="rglru",
    )(z, z, conv_w, conv_b, w_dense, b_dense, lam)


def _merge_kernel(x_ref, b0_ref, b1_ref, b2_ref, b3_ref, gate_ref, gl_ref, gc_ref, wb_ref, wo_ref,
                  lng_ref, lnb_ref, o_ref, *, ctx_len, tm, alpha):
    t = pl.program_id(1)
    merged = None
    for j, br in enumerate((b0_ref, b1_ref, b2_ref, b3_ref)):
        gate = jax.nn.sigmoid(gate_ref[0, :, j * D_MODEL:(j + 1) * D_MODEL].astype(F32))
        term = gate * _dot(br[0], wb_ref[j])
        merged = term if merged is None else merged + term
    y = _dot(merged.astype(BF16), wo_ref[...])
    rows = _row_ids((tm, 1), t * tm)
    g1 = jnp.where(rows < ctx_len, gc_ref[...], gl_ref[0])
    o_ref[0] = _ln0(alpha * x_ref[0] + g1 * y) * lng_ref[...] + lnb_ref[...]


def _merge(h, branches, z, g_l, g_c, w_branch, w_out, ln_g, ln_b, ctx_len, alpha):
    b, s, d = h.shape
    tm = _pick_tile(s, (256, 128))
    kern = functools.partial(_merge_kernel, ctx_len=ctx_len, tm=tm, alpha=alpha)
    bw = BRANCH_WIDTH

    def tile(width, blk=0):
        return pl.BlockSpec((1, tm, width), lambda i, t: (i, t, blk))

    def full(a):
        nd = a.ndim
        return pl.BlockSpec(a.shape, lambda i, t: (0,) * nd)

    return pl.pallas_call(
        kern,
        grid=(b, s // tm),
        in_specs=[tile(d), tile(bw), tile(bw), tile(bw), tile(bw), tile(N_BRANCH * d, BLK_MG),
                  pl.BlockSpec((1, 1, d), lambda i, t: (i, 0, 0)), full(g_c),
                  full(w_branch), full(w_out), full(ln_g), full(ln_b)],
        out_specs=tile(d),
        out_shape=jax.ShapeDtypeStruct((b, s, d), F32),
        compiler_params=pltpu.CompilerParams(
            dimension_semantics=("parallel", "parallel"), vmem_limit_bytes=VMEM_LIMIT),
        name="merge",
    )(h, *branches, z, g_l, g_c, w_branch, w_out, ln_g, ln_b)


FFN_ROWS = 64
FFN_COLS = 256


def _ffn_kernel(x_ref, xp_ref, xn_ref, shl_ref, scl_ref, gl_ref, shc_ref, scc_ref, gc_ref,
                wu_ref, cw_ref, cb_ref, wd_ref, lng_ref, lnb_ref,
                o_ref, xm_ref, uv_ref, ug_ref, act_ref, *, ctx_len, seq, tm, alpha):
    d_ff = wd_ref.shape[0]
    fc = _pick_tile(d_ff, (FFN_COLS, LANES))
    rb = _pick_tile(tm, (FFN_ROWS, 8))
    r0 = pl.program_id(1) * tm

    x = x_ref[0]
    x_ext = jnp.concatenate([xp_ref[0], x, xn_ref[0]], axis=0)
    xm = _modulated(x_ext, _row_ids((tm + 16, 1), r0 - 8), ctx_len,
                    shl_ref[0], scl_ref[0], shc_ref[...], scc_ref[...])
    xm_ref[...] = xm.astype(BF16)

    rows = _row_ids((tm, 1), r0)
    m_up = jnp.where((rows == 0) | (rows == ctx_len), 0.0, 1.0)
    m_dn = jnp.where((rows == ctx_len - 1) | (rows == seq - 1), 0.0, 1.0)

    def conv(u_ref, a, mu, md, cw, cb):
        up = u_ref[7 + a:7 + a + rb, :]
        mid = u_ref[8 + a:8 + a + rb, :]
        dn = u_ref[9 + a:9 + a + rb, :]
        return cw[0:1] * (mu * up) + cw[1:2] * mid + cw[2:3] * (md * dn) + cb

    for j in range(d_ff // fc):
        cv = slice(j * fc, (j + 1) * fc)
        cg = slice(d_ff + j * fc, d_ff + (j + 1) * fc)
        uv_ref[...] = _dot(xm_ref[...], wu_ref[:, cv])
        ug_ref[...] = _dot(xm_ref[...], wu_ref[:, cg])
        for a in range(0, tm, rb):
            mu, md = m_up[a:a + rb], m_dn[a:a + rb]
            val = conv(uv_ref, a, mu, md, cw_ref[:, cv], cb_ref[:, cv])
            gate = conv(ug_ref, a, mu, md, cw_ref[:, cg], cb_ref[:, cg])
            act_ref[a:a + rb, cv] = (_silu(gate) * val).astype(BF16)

    f = _dot(act_ref[...], wd_ref[...])
    g2 = jnp.where(rows < ctx_len, gc_ref[...], gl_ref[0])
    o_ref[0] = _ln0(alpha * x + g2 * f) * lng_ref[...] + lnb_ref[...]


def _conv_ffn(h, mods_l, mods_c, w_up, conv_w, conv_b, w_down, ln_g, ln_b, ctx_len, alpha):
    b, s, d = h.shape
    d_ff = w_down.shape[0]
    tm = _pick_tile(s, (768, 384, 256, 128))
    fc = _pick_tile(d_ff, (FFN_COLS, LANES))
    kern = functools.partial(_ffn_kernel, ctx_len=ctx_len, seq=s, tm=tm, alpha=alpha)
    nb8 = s // 8

    def vec_l():
        return pl.BlockSpec((1, 1, d), lambda i, t: (i, 0, 0))

    def resident(a):
        nd = a.ndim
        return pl.BlockSpec(a.shape, lambda i, t: (0,) * nd, pipeline_mode=pl.Buffered(1))

    sh_l, sc_l, g_l = mods_l
    sh_c, sc_c, g_c = mods_c
    return pl.pallas_call(
        kern,
        grid=(b, s // tm),
        in_specs=[pl.BlockSpec((1, tm, d), lambda i, t: (i, t, 0)),
                  pl.BlockSpec((1, 8, d), lambda i, t: (i, jnp.maximum(t * (tm // 8) - 1, 0), 0)),
                  pl.BlockSpec((1, 8, d), lambda i, t: (i, jnp.minimum((t + 1) * (tm // 8), nb8 - 1), 0)),
                  vec_l(), vec_l(), vec_l(), resident(sh_c), resident(sc_c), resident(g_c),
                  resident(w_up), resident(conv_w), resident(conv_b), resident(w_down),
                  resident(ln_g), resident(ln_b)],
        out_specs=pl.BlockSpec((1, tm, d), lambda i, t: (i, t, 0)),
        out_shape=jax.ShapeDtypeStruct((b, s, d), F32),
        scratch_shapes=[pltpu.VMEM((tm + 16, d), BF16),
                        pltpu.VMEM((tm + 16, fc), F32), pltpu.VMEM((tm + 16, fc), F32),
                        pltpu.VMEM((tm, d_ff), BF16)],
        compiler_params=pltpu.CompilerParams(
            dimension_semantics=("parallel", "parallel"), vmem_limit_bytes=VMEM_LIMIT),
        name="conv_ffn",
    )(h, h, h, sh_l, sc_l, g_l, sh_c, sc_c, g_c, w_up, conv_w, conv_b, w_down, ln_g, ln_b)


def _rope_tables(seq, ctx_len):
    quarter = RET_DK // 4
    inv_freq = ROPE_BASE ** (-jnp.arange(quarter, dtype=F32) / quarter)
    p = jnp.arange(seq - ctx_len)
    row = (p // GRID_W).astype(F32)
    col = (p % GRID_W).astype(F32)
    ang_r = row[:, None] * inv_freq[None, :]
    ang_c = col[:, None] * inv_freq[None, :]
    cos = jnp.concatenate([jnp.cos(ang_r)] * 2 + [jnp.cos(ang_c)] * 2, axis=-1)
    sin = jnp.concatenate([-jnp.sin(ang_r), jnp.sin(ang_r), -jnp.sin(ang_c), jnp.sin(ang_c)], axis=-1)
    cos = jnp.concatenate([jnp.ones((ctx_len, RET_DK), F32), cos], axis=0)
    sin = jnp.concatenate([jnp.zeros((ctx_len, RET_DK), F32), sin], axis=0)
    ks = RET_DK ** -0.5
    return jnp.concatenate([cos, cos * ks], axis=-1), jnp.concatenate([sin, sin * ks], axis=-1)


def _block_diag(w):
    nb, k, _ = w.shape
    eye = jnp.eye(nb, dtype=w.dtype)
    return (eye[:, None, :, None] * w[:, :, None, :]).reshape(nb * k, nb * k)


def kernel(x, c, ctx, c_ctx, w_mod, b_mod, w_in, hg_lb, hg_norm, gdn_conv, gdn_a_log, gdn_dt_bias,
           gdn_norm, lru_conv, lru_conv_b, lru_w_a, lru_b_a, lru_w_i, lru_b_i, lru_lam, w_branch,
           w_out, ln_mix_g, ln_mix_b, w_up, ffn_conv, ffn_conv_b, w_down, ln_ffn_g, ln_ffn_b):
    depth = w_in.shape[0]
    batch, seq_l, d = x.shape
    ctx_len = ctx.shape[1]
    seq = ctx_len + seq_l
    alpha = (2.0 * depth) ** 0.25

    perm, n_in = _proj_perm()
    w_in_p = _take_columns(w_in.astype(BF16), perm, n_in)
    gdn_conv_p = _take_columns(gdn_conv, _gdn_conv_perm(), gdn_conv.shape[-1])
    lb = jnp.cumsum(jax.nn.softmax(hg_lb.astype(F32), axis=0), axis=0)
    lb = lb - lb[:1]
    log_gamma = jnp.log1p(-jnp.exp2(-5.0 - jnp.arange(RET_HEADS, dtype=F32)))
    lg_rows = jnp.broadcast_to(log_gamma[:, None, None], (RET_HEADS, 1, LANES))
    cos_t, sin_t = _rope_tables(seq, ctx_len)
    w_branch_b = w_branch.astype(BF16)
    w_out_b = w_out.astype(BF16)
    w_up_b = w_up.astype(BF16)
    w_down_b = w_down.astype(BF16)

    cc = jnp.concatenate([c, c_ctx[None, :], jnp.zeros((7, d), F32)], axis=0)
    mod = _modulation(cc, w_mod, b_mod)

    h = jnp.concatenate([ctx, x], axis=1)
    for i in range(depth):
        ml = mod[i, :batch].reshape(batch, N_MOD, 1, d)
        mc = mod[i, batch].reshape(N_MOD, 1, d)
        sh1, sc1, g1, sh2, sc2, g2 = (ml[:, k] for k in range(N_MOD))
        csh1, csc1, cg1, csh2, csc2, cg2 = (mc[k] for k in range(N_MOD))

        z = _in_projection(h, sh1, sc1, csh1, csc1, w_in_p[i], ctx_len)

        ret = _retention(z, lg_rows, cos_t, sin_t, ctx_len)
        hg = _hgrn2(z, lb[i, 0][None, :], lb[i, 1][None, :], hg_norm[i][None, :], ctx_len)
        par = jnp.stack([gdn_a_log[i, 0], gdn_dt_bias[i, 0], gdn_a_log[i, 1], gdn_dt_bias[i, 1]], axis=1)
        par = jnp.broadcast_to(jnp.pad(par, ((0, 0), (0, 4)))[:, :, None], (GDN_HEADS, 8, LANES))
        gd = _gdn(z, gdn_conv_p[i], par, gdn_norm[i][None, :], ctx_len)
        w_dense = jnp.stack([
            jnp.concatenate([_block_diag(lru_w_a[i, dd]), _block_diag(lru_w_i[i, dd])], axis=1)
            for dd in range(2)]).astype(BF16)
        b_dense = jnp.stack([jnp.concatenate([lru_b_a[i, dd], lru_b_i[i, dd]])[None, :] for dd in range(2)])
        lr = _rglru(z, lru_conv[i], lru_conv_b[i][None, :], w_dense, b_dense,
                    lru_lam[i].astype(F32)[:, None, :], ctx_len)

        h = _merge(h, (ret, hg, gd, lr), z, g1, cg1, w_branch_b[i], w_out_b[i],
                   ln_mix_g[i][None, :], ln_mix_b[i][None, :], ctx_len, alpha)
        h = _conv_ffn(h, (sh2, sc2, g2), (csh2, csc2, cg2), w_up_b[i], ffn_conv[i],
                      ffn_conv_b[i][None, :], w_down_b[i], ln_ffn_g[i][None, :], ln_ffn_b[i][None, :],
                      ctx_len, alpha)
    return h[:, ctx_len:]
```

```python
import functools
import math

import numpy as np
import jax
import jax.numpy as jnp
from jax import lax
from jax.experimental import pallas as pl
from jax.experimental.pallas import tpu as pltpu

F32 = jnp.float32
BF16 = jnp.bfloat16

D_MODEL = 1024
GRID_W = 64
NORM_EPS = 1e-6
ROPE_BASE = 10000.0
RET_HEADS, RET_DK, RET_DV = 4, 64, 128
HG_HEADS, HG_DK, HG_DV = 4, 128, 128
GDN_HEADS, GDN_DK, GDN_DV, GDN_CONV = 4, 64, 128, 4
LRU_WIDTH, LRU_BLOCKS, LRU_CONV, LRU_C = 512, 8, 4, 8.0
N_BRANCH, BRANCH_WIDTH = 4, 512
D_FF, FFN_CONV = 2816, 3
N_MOD = 6

LANES = 128
HALO = 16
VMEM_LIMIT = 56 * 1024 * 1024

BLK_MG = 0
BLK_LRU_X = 32
BLK_LRU_G = 36
BLK_HG_Q = 40
BLK_HG_FF = 44
BLK_HG_FB = 48
BLK_HG_I = 52
BLK_HG_G = 56
BLK_RET_QK = 60
BLK_RET_V = 64
BLK_RET_G = 68
BLK_GDN_QK = 72
BLK_GDN_V = 76
BLK_GDN_G = 80
BLK_GDN_MISC = 84
N_BLK = 85
N_PROJ = N_BLK * LANES

RET_CHUNK = 256
GLA_CHUNK = 64
LRU_ROWS = 256
GDN_GROUP = 3


def _dot(a, b):
    return jnp.dot(a, b, preferred_element_type=F32)


def _dot_nt(a, b):
    return lax.dot_general(a, b, (((1,), (1,)), ((), ())), preferred_element_type=F32)


def _dot_tn(a, b):
    return lax.dot_general(a, b, (((0,), (0,)), ((), ())), preferred_element_type=F32)


def _silu(x):
    return x * jax.nn.sigmoid(x)


def _split3(x):
    hi = x.astype(BF16)
    r1 = x - hi.astype(F32)
    mid = r1.astype(BF16)
    lo = (r1 - mid.astype(F32)).astype(BF16)
    return hi, mid, lo


def _ln0(x):
    mu = jnp.mean(x, axis=-1, keepdims=True)
    xc = x - mu
    var = jnp.mean(xc * xc, axis=-1, keepdims=True)
    return xc * lax.rsqrt(var + NORM_EPS)


def _row_ids(shape, start):
    return start + lax.broadcasted_iota(jnp.int32, shape, 0)


def _chunk_order(it, n_ctx, n_all):
    cb = jnp.where(it < n_ctx, n_ctx - 1 - it, n_all + n_ctx - 1 - it)
    return it, cb


def _level_structs(c):
    t = np.arange(c)
    j = t[None, :]
    segs, masks = [], []
    half = c // 2
    while half >= 1:
        blk = t // (2 * half)
        ref = blk * 2 * half + half - 1
        upper = (t - blk * 2 * half) >= half
        seg_u = (j > ref[:, None]) & (j <= t[:, None])
        seg_l = (j > t[:, None]) & (j <= ref[:, None])
        segs.append(np.where(upper[:, None], seg_u, seg_l))
        masks.append((blk[:, None] == blk[None, :]) & upper[:, None] & (~upper)[None, :])
        half //= 2
    return segs, masks


def _cum_rows(c):
    t = np.arange(c)
    j = t[None, :]
    incl = j <= t[:, None]
    rest = j > t[:, None]
    ones = np.ones((8, c), bool)
    return [incl, rest, ones]


def _flip2(m):
    return m[::-1, ::-1]


@functools.lru_cache(maxsize=None)
def _hg_consts(c):
    segs, masks = _level_structs(c)
    rows_f = segs + _cum_rows(c)
    rows_b = [_flip2(m) for m in segs] + [_flip2(m) for m in _cum_rows(c)[:2]] + _cum_rows(c)[2:]
    masks = masks + [np.eye(c, dtype=bool)]
    a_f = np.concatenate(rows_f, axis=0).astype(np.float32)
    a_b = np.concatenate(rows_b, axis=0).astype(np.float32)
    a3 = np.stack([np.tile(a_f, (1, 3)), np.tile(a_b, (1, 3))])

    def bd4(m_f):
        out = np.zeros((4 * c, 4 * c), np.float32)
        for b in range(4):
            out[b * c:(b + 1) * c, b * c:(b + 1) * c] = m_f if b % 2 == 0 else _flip2(m_f)
        return out

    return a3, np.stack([bd4(m) for m in masks])


@functools.lru_cache(maxsize=None)
def _gdn_consts(c):
    _, masks = _level_structs(c)
    masks = masks[::-1]
    t = np.arange(c)
    incl = t[None, :] <= t[:, None]

    def bd4(m_f):
        out = np.zeros((4 * c, 4 * c), np.float32)
        for b in range(4):
            out[b * c:(b + 1) * c, b * c:(b + 1) * c] = m_f if b % 2 == 0 else _flip2(m_f)
        return out

    ones = np.ones((c, c), bool)
    a = np.concatenate([bd4(incl), bd4(ones)], axis=0)
    lv = np.stack([bd4(m) for m in masks])
    return a, lv, bd4(incl)


def _proj_perm():
    names = (('ret_q', 256), ('ret_k', 256), ('ret_v', 512), ('ret_g', 512),
             ('hg_q', 512), ('hg_f_fwd', 512), ('hg_f_bwd', 512), ('hg_i', 512), ('hg_g', 512),
             ('gdn_qkv', 1024), ('gdn_a', 8), ('gdn_b', 8), ('gdn_g', 512),
             ('lru_x', 512), ('lru_gate', 512), ('merge_gate', 4096))
    off, o = {}, 0
    for name, w in names:
        off[name] = o
        o += w
    n_in = o
    perm = np.full((N_PROJ,), n_in, np.int64)

    def put(blk, src, width):
        perm[blk * LANES: blk * LANES + width] = np.arange(src, src + width)

    put(BLK_MG, off['merge_gate'], 4096)
    put(BLK_LRU_X, off['lru_x'], 512)
    put(BLK_LRU_G, off['lru_gate'], 512)
    put(BLK_HG_Q, off['hg_q'], 512)
    put(BLK_HG_FF, off['hg_f_fwd'], 512)
    put(BLK_HG_FB, off['hg_f_bwd'], 512)
    put(BLK_HG_I, off['hg_i'], 512)
    put(BLK_HG_G, off['hg_g'], 512)
    for h in range(RET_HEADS):
        perm[(BLK_RET_QK + h) * LANES: (BLK_RET_QK + h) * LANES + 64] = off['ret_q'] + 64 * h + np.arange(64)
        perm[(BLK_RET_QK + h) * LANES + 64: (BLK_RET_QK + h + 1) * LANES] = off['ret_k'] + 64 * h + np.arange(64)
    put(BLK_RET_V, off['ret_v'], 512)
    put(BLK_RET_G, off['ret_g'], 512)
    gq = off['gdn_qkv']
    for h in range(GDN_HEADS):
        perm[(BLK_GDN_QK + h) * LANES: (BLK_GDN_QK + h) * LANES + 64] = gq + 64 * h + np.arange(64)
        perm[(BLK_GDN_QK + h) * LANES + 64: (BLK_GDN_QK + h + 1) * LANES] = gq + 256 + 64 * h + np.arange(64)
    put(BLK_GDN_V, gq + 512, 512)
    put(BLK_GDN_G, off['gdn_g'], 512)
    put(BLK_GDN_MISC, off['gdn_a'], 8)
    perm[BLK_GDN_MISC * LANES + 8: BLK_GDN_MISC * LANES + 16] = off['gdn_b'] + np.arange(8)
    return perm, n_in


def _take_columns(w, perm, n_src):
    pieces, start = [], 0
    for i in range(1, len(perm) + 1):
        pad = perm[start] == n_src
        if i < len(perm) and ((pad and perm[i] == n_src) or
                              (not pad and perm[i] != n_src and perm[i] == perm[i - 1] + 1)):
            continue
        if pad:
            pieces.append(jnp.zeros(w.shape[:-1] + (i - start,), w.dtype))
        else:
            pieces.append(w[..., int(perm[start]):int(perm[start]) + i - start])
        start = i
    return jnp.concatenate(pieces, axis=-1)


def _gdn_conv_perm():
    p = np.zeros((1024,), np.int64)
    for h in range(GDN_HEADS):
        p[128 * h: 128 * h + 64] = 64 * h + np.arange(64)
        p[128 * h + 64: 128 * h + 128] = 256 + 64 * h + np.arange(64)
    p[512:] = 512 + np.arange(512)
    return p


def _mod_kernel(c_ref, w_ref, b_ref, o_ref):
    s = _silu(c_ref[...])
    o_ref[0] = jnp.dot(s, w_ref[0], preferred_element_type=F32,
                       precision=lax.Precision.HIGHEST) + b_ref[0]


def _modulation(cc, w_mod, b_mod):
    depth, d, n = w_mod.shape
    rows = cc.shape[0]
    tn = 1024
    return pl.pallas_call(
        _mod_kernel,
        grid=(depth, n // tn),
        in_specs=[pl.BlockSpec((rows, d), lambda l, j: (0, 0)),
                  pl.BlockSpec((1, d, tn), lambda l, j: (l, 0, j)),
                  pl.BlockSpec((1, 1, tn), lambda l, j: (l, 0, j))],
        out_specs=pl.BlockSpec((1, rows, tn), lambda l, j: (l, 0, j)),
        out_shape=jax.ShapeDtypeStruct((depth, rows, n), F32),
        compiler_params=pltpu.CompilerParams(vmem_limit_bytes=VMEM_LIMIT),
        name="modulation",
    )(cc, w_mod, b_mod.reshape(depth, 1, n))


def _modulated(x, rows, ctx_len, sh_l, sc_l, sh_c, sc_c):
    is_ctx = rows < ctx_len
    scale = jnp.where(is_ctx, sc_c, sc_l)
    shift = jnp.where(is_ctx, sh_c, sh_l)
    return _ln0(x) * (1.0 + scale) + shift


def _inproj_kernel(x_ref, shl_ref, scl_ref, shc_ref, scc_ref, w_ref, o_ref, xm_ref, *, ctx_len, tm):
    t = pl.program_id(1)

    @pl.when(pl.program_id(2) == 0)
    def _():
        rows = _row_ids((tm, 1), t * tm)
        xm = _modulated(x_ref[0], rows, ctx_len, shl_ref[0], scl_ref[0], shc_ref[...], scc_ref[...])
        xm_ref[...] = xm.astype(BF16)

    o_ref[0] = _dot(xm_ref[...], w_ref[...]).astype(o_ref.dtype)


def _pick_tile(n, cands):
    for c in cands:
        if n % c == 0:
            return c
    return n


def _in_projection(h, sh_l, sc_l, sh_c, sc_c, w_p, ctx_len):
    b, s, d = h.shape
    n = w_p.shape[1]
    tm = _pick_tile(s, (1152, 768, 512, 384, 256, 128))
    tn = _pick_tile(n, (2176, 1280, 640, 128))
    kern = functools.partial(_inproj_kernel, ctx_len=ctx_len, tm=tm)
    return pl.pallas_call(
        kern,
        grid=(b, s // tm, n // tn),
        in_specs=[pl.BlockSpec((1, tm, d), lambda i, t, j: (i, t, 0)),
                  pl.BlockSpec((1, 1, d), lambda i, t, j: (i, 0, 0)),
                  pl.BlockSpec((1, 1, d), lambda i, t, j: (i, 0, 0)),
                  pl.BlockSpec((1, d), lambda i, t, j: (0, 0)),
                  pl.BlockSpec((1, d), lambda i, t, j: (0, 0)),
                  pl.BlockSpec((d, tn), lambda i, t, j: (0, j))],
        out_specs=pl.BlockSpec((1, tm, tn), lambda i, t, j: (i, t, j)),
        out_shape=jax.ShapeDtypeStruct((b, s, n), BF16),
        scratch_shapes=[pltpu.VMEM((tm, d), BF16)],
        compiler_params=pltpu.CompilerParams(
            dimension_semantics=("parallel", "parallel", "arbitrary"),
            vmem_limit_bytes=VMEM_LIMIT),
        name="in_projection",
    )(h, sh_l, sc_l, sh_c, sc_c, w_p)


def _ret_kernel(lg_ref, qk_ref, v_ref, g_ref, cos_ref, sin_ref, o_ref,
                p_ref, q2_ref, ds_ref, sp_ref, *, ctx_len, seq, c):
    n_all, n_ctx = seq // c, ctx_len // c
    dk = RET_DK
    lane = lax.broadcasted_iota(jnp.int32, (c, LANES), 1)
    lg = lg_ref[0]
    pos = lax.broadcasted_iota(jnp.int32, (c, LANES), 0).astype(F32)
    is_q = lane < dk
    fac_f = jnp.exp(jnp.where(is_q, pos + 1.0, c - 1.0 - pos) * lg)
    fac_b = jnp.exp(jnp.where(is_q, c - pos, pos) * lg)
    rr = lax.broadcasted_iota(jnp.int32, (c, c), 0)
    cc = lax.broadcasted_iota(jnp.int32, (c, c), 1)
    dist = jnp.abs(rr - cc).astype(F32)
    dmat = jnp.exp(dist * lg[:, :1]) * jnp.where(rr == cc, 2.0, 1.0)
    gc = jnp.exp(lg * float(c))

    per_step = _pick_tile(n_all, (3, 1))

    def over_chunks(fn):
        def body(gi, carry):
            for j in range(per_step):
                fn(gi * per_step + j)
            return carry
        lax.fori_loop(0, n_all // per_step, body, 0)

    def local(ci):
        rows = pl.ds(pl.multiple_of(ci * c, c), c)
        x = qk_ref[0, rows, :].astype(F32)
        swapped = jnp.where(lane % 32 < 16, pltpu.roll(x, LANES - 16, 1), pltpu.roll(x, 16, 1))
        qk = x * cos_ref[rows, :] + swapped * sin_ref[rows, :]
        p_ref[ci] = (_dot_nt(qk[:, :dk].astype(BF16), qk[:, dk:].astype(BF16)) * dmat).astype(BF16)
        qk_f = qk * fac_f
        qk_b = qk * fac_b
        q2_ref[rows, :] = jnp.where(is_q, qk_f, pltpu.roll(qk_b, dk, 1)).astype(BF16)
        k2 = jnp.where(is_q, pltpu.roll(qk_f, dk, 1), qk_b).astype(BF16)
        ds_ref[ci] = _dot_tn(k2, v_ref[0, rows, :])

    over_chunks(local)

    def step(it, carry):
        s_f, s_b = carry
        cf, cb = _chunk_order(it, n_ctx, n_all)
        sp_ref[cf, 0:dk, :] = s_f.astype(BF16)
        sp_ref[cb, dk:2 * dk, :] = s_b.astype(BF16)
        return gc * s_f + ds_ref[cf, 0:dk, :], gc * s_b + ds_ref[cb, dk:2 * dk, :]

    zero = jnp.zeros((dk, RET_DV), F32)
    lax.fori_loop(0, n_all, step, (zero, zero))

    def readout(ci):
        rows = pl.ds(pl.multiple_of(ci * c, c), c)
        lhs = jnp.concatenate([p_ref[ci], q2_ref[rows, :]], axis=1)
        rhs = jnp.concatenate([v_ref[0, rows, :], sp_ref[ci]], axis=0)
        o = _ln0(_dot(lhs, rhs))
        o_ref[0, rows, :] = (o * _silu(g_ref[0, rows, :].astype(F32))).astype(o_ref.dtype)

    over_chunks(readout)


def _retention(z, lg_rows, cos_t, sin_t, ctx_len):
    b, s, _ = z.shape
    c = _pick_tile(math.gcd(ctx_len, s), (RET_CHUNK, 128))
    n = s // c
    kern = functools.partial(_ret_kernel, ctx_len=ctx_len, seq=s, c=c)

    def col(blk):
        return pl.BlockSpec((1, s, LANES), lambda i, h: (i, 0, blk + h))

    return pl.pallas_call(
        kern,
        grid=(b, RET_HEADS),
        in_specs=[pl.BlockSpec((1, 1, LANES), lambda i, h: (h, 0, 0)),
                  col(BLK_RET_QK), col(BLK_RET_V), col(BLK_RET_G),
                  pl.BlockSpec((s, LANES), lambda i, h: (0, 0)),
                  pl.BlockSpec((s, LANES), lambda i, h: (0, 0))],
        out_specs=pl.BlockSpec((1, s, LANES), lambda i, h: (i, 0, h)),
        out_shape=jax.ShapeDtypeStruct((b, s, RET_HEADS * RET_DV), BF16),
        scratch_shapes=[pltpu.VMEM((n, c, c), BF16),
                        pltpu.VMEM((s, LANES), BF16),
                        pltpu.VMEM((n, 2 * RET_DK, RET_DV), F32),
                        pltpu.VMEM((n, 2 * RET_DK, RET_DV), BF16)],
        compiler_params=pltpu.CompilerParams(
            dimension_semantics=("parallel", "parallel"), vmem_limit_bytes=VMEM_LIMIT),
        name="retention",
    )(lg_rows, z, z, z, cos_t, sin_t)


def _hg_kernel(q_ref, ff_ref, fb_ref, i_ref, g_ref, lbf_ref, lbb_ref, nw_ref, a3_ref, m_ref, o_ref,
               acc_ref, q2_ref, ds_ref, sp_ref, dec_ref, *, ctx_len, seq):
    c = GLA_CHUNK
    n_all, n_ctx = seq // c, ctx_len // c
    nl = m_ref.shape[0] - 1
    dk = HG_DK

    def stack4(x_f, x_b):
        return jnp.concatenate([x_f[:c], x_b[:c], x_f[c:], x_b[c:]], axis=0)

    def local(gi, carry):
        rows2 = pl.ds(pl.multiple_of(gi * 2 * c, 2 * c), 2 * c)
        q = _silu(q_ref[0, rows2, :].astype(F32))
        v = i_ref[0, rows2, :]
        kk, r = [], []
        for d, (f_ref, lb_ref) in enumerate(((ff_ref, lbf_ref), (fb_ref, lbb_ref))):
            lb = lb_ref[...]
            f = lb + (1.0 - lb) * jax.nn.sigmoid(f_ref[0, rows2, :].astype(F32))
            g = jnp.log(f)
            kk.append(1.0 - f)
            r.append([_dot(a3_ref[d], jnp.concatenate(_split3(g[j * c:(j + 1) * c]), axis=0))
                      for j in range(2)])

        def rows_of(lo, n=c):
            return [r[d][j][lo:lo + n] for j in range(2) for d in range(2)]

        q4 = stack4(q, q)
        k4 = stack4(kk[0], kk[1])
        scores = m_ref[nl] * _dot_nt(q4.astype(BF16), k4.astype(BF16))
        for l in range(nl):
            e = jnp.exp(jnp.concatenate(rows_of(l * c), axis=0))
            scores = scores + m_ref[l] * _dot_nt((q4 * e).astype(BF16), (k4 * e).astype(BF16))
        v4 = jnp.concatenate([v[:c], v[:c], v[c:], v[c:]], axis=0)
        o4 = _dot(scores.astype(BF16), v4)
        acc_ref[rows2, :] = jnp.concatenate([o4[:c] + o4[c:2 * c], o4[2 * c:3 * c] + o4[3 * c:]], axis=0)

        ecum = [jnp.exp(x) for x in rows_of(nl * c)]
        erest = [jnp.exp(x) for x in rows_of((nl + 1) * c)]
        etot = [jnp.exp(x) for x in rows_of((nl + 2) * c, 8)]
        for j in range(2):
            ci = 2 * gi + j
            qj = q[j * c:(j + 1) * c]
            rows = pl.ds(pl.multiple_of(ci * c, c), c)
            q2_ref[rows, :] = jnp.concatenate([qj * ecum[2 * j], qj * ecum[2 * j + 1]], axis=1).astype(BF16)
            k2 = jnp.concatenate([kk[0][j * c:(j + 1) * c] * erest[2 * j],
                                  kk[1][j * c:(j + 1) * c] * erest[2 * j + 1]], axis=1).astype(BF16)
            ds_ref[ci] = _dot_tn(v[j * c:(j + 1) * c], k2)
            dec_ref[ci] = jnp.concatenate([etot[2 * j], etot[2 * j + 1]], axis=1)
        return carry

    lax.fori_loop(0, n_all // 2, local, 0)

    def step(it, carry):
        st_f, st_b = carry
        cf, cb = _chunk_order(it, n_ctx, n_all)
        sp_ref[cf, :, 0:dk] = st_f.astype(BF16)
        sp_ref[cb, :, dk:2 * dk] = st_b.astype(BF16)
        st_f = st_f * dec_ref[cf][0:1, 0:dk] + ds_ref[cf, :, 0:dk]
        st_b = st_b * dec_ref[cb][0:1, dk:2 * dk] + ds_ref[cb, :, dk:2 * dk]
        return st_f, st_b

    zero = jnp.zeros((HG_DV, dk), F32)
    lax.fori_loop(0, n_all, step, (zero, zero))

    per_step = _pick_tile(n_all, (4, 2))

    def readout(gi, carry):
        for j in range(per_step):
            ci = per_step * gi + j
            rows = pl.ds(pl.multiple_of(ci * c, c), c)
            o = acc_ref[rows, :] + _dot_nt(q2_ref[rows, :], sp_ref[ci])
            o = o * lax.rsqrt(jnp.mean(o * o, axis=-1, keepdims=True) + NORM_EPS)
            o_ref[0, rows, :] = (o * nw_ref[...] * _silu(g_ref[0, rows, :].astype(F32))).astype(o_ref.dtype)
        return carry

    lax.fori_loop(0, n_all // per_step, readout, 0)


def _hgrn2(z, lb_f, lb_b, norm_w, ctx_len):
    b, s, _ = z.shape
    c = GLA_CHUNK
    n = s // c
    assert n % 2 == 0
    a3, masks = _hg_consts(c)
    a3 = jnp.asarray(a3, BF16)
    masks = jnp.asarray(masks, F32)
    kern = functools.partial(_hg_kernel, ctx_len=ctx_len, seq=s)

    def col(blk):
        return pl.BlockSpec((1, s, LANES), lambda i, h: (i, 0, blk + h))

    def full(a):
        nd = a.ndim
        return pl.BlockSpec(a.shape, lambda i, h: (0,) * nd)

    return pl.pallas_call(
        kern,
        grid=(b, HG_HEADS),
        in_specs=[col(BLK_HG_Q), col(BLK_HG_FF), col(BLK_HG_FB), col(BLK_HG_I), col(BLK_HG_G),
                  pl.BlockSpec((1, LANES), lambda i, h: (0, h)),
                  pl.BlockSpec((1, LANES), lambda i, h: (0, h)),
                  pl.BlockSpec((1, LANES), lambda i, h: (0, 0)),
                  full(a3), full(masks)],
        out_specs=pl.BlockSpec((1, s, LANES), lambda i, h: (i, 0, h)),
        out_shape=jax.ShapeDtypeStruct((b, s, HG_HEADS * HG_DV), BF16),
        scratch_shapes=[pltpu.VMEM((s, HG_DV), F32),
                        pltpu.VMEM((s, 2 * HG_DK), BF16),
                        pltpu.VMEM((n, HG_DV, 2 * HG_DK), F32),
                        pltpu.VMEM((n, HG_DV, 2 * HG_DK), BF16),
                        pltpu.VMEM((n, 8, 2 * HG_DK), F32)],
        compiler_params=pltpu.CompilerParams(
            dimension_semantics=("parallel", "parallel"), vmem_limit_bytes=VMEM_LIMIT),
        name="hgrn2",
    )(z, z, z, z, z, lb_f, lb_b, norm_w, a3, masks)


def _conv_rows(ref, r0, c, seq, ctx_len, w, taps_left, cols=slice(None)):
    lo = pl.multiple_of(jnp.maximum(r0 - HALO, 0), HALO)
    hi = pl.multiple_of(jnp.minimum(r0 + c, seq - HALO), HALO)
    xh = jnp.concatenate([ref[0, pl.ds(lo, HALO), cols].astype(F32),
                          ref[0, pl.ds(pl.multiple_of(r0, HALO), c), cols].astype(F32),
                          ref[0, pl.ds(hi, HALO), cols].astype(F32)], axis=0)
    rows = _row_ids((c + 2 * HALO, 1), r0 - HALO)
    seg_lo = jnp.where(r0 < ctx_len, 0, ctx_len)
    seg_hi = jnp.where(r0 < ctx_len, ctx_len, seq)
    xh = jnp.where((rows >= seg_lo) & (rows < seg_hi), xh, 0.0)
    out = None
    for j in range(w.shape[0]):
        start = HALO + j - taps_left
        term = xh[start:start + c] * w[j:j + 1]
        out = term if out is None else out + term
    return out


GDN_PAIR = 2


def _gdn_pair_kernel(qk_ref, v_ref, g_ref, misc_ref, cw_qk_ref, cw_v_ref, par_ref, nw_ref,
                     a3_ref, lv_ref, incl_ref, o_ref,
                     qkn_ref, va_ref, acc_ref, lhs1_ref, lhs2_ref, u0_ref, el_ref,
                     base_ref, xs_ref, t_ref, rhs_ref, *, ctx_len, seq):
    c = GLA_CHUNK
    cq = 4 * c
    n_all, n_ctx = seq // c, ctx_len // c
    nq = n_all // 2
    eye = (lax.broadcasted_iota(jnp.int32, (cq, cq), 0)
           == lax.broadcasted_iota(jnp.int32, (cq, cq), 1)).astype(F32)
    cp = 2 * c if ctx_len % (2 * c) == 0 else c
    lane_p = lax.broadcasted_iota(jnp.int32, (cp, LANES), 1)
    is_q = lane_p < GDN_DK

    rr = lax.broadcasted_iota(jnp.int32, (LANES, 4 * LANES), 0)
    cc4 = lax.broadcasted_iota(jnp.int32, (LANES, 4 * LANES), 1) // LANES
    lane = lax.broadcasted_iota(jnp.int32, (cq, LANES), 1)
    low = lane < GDN_DK
    rowb = lax.broadcasted_iota(jnp.int32, (cq, LANES), 0) // c
    own_half = (rowb % 2 == 0) == low
    e_hi = (lane == 0).astype(BF16)
    lane_k = lax.broadcasted_iota(jnp.int32, (GDN_DK, LANES), 1)
    eye_k = (lax.broadcasted_iota(jnp.int32, (GDN_DK, LANES), 0) == lane_k).astype(BF16)
    half_t = [lane_k < GDN_DK, lane_k >= GDN_DK]

    def stack4(x_f, x_b):
        return jnp.concatenate([x_f[:c], x_b[:c], x_f[c:], x_b[c:]], axis=0)

    def step_of(ci, d):
        return ci if d == 0 else jnp.where(ci < n_ctx, n_ctx - 1 - ci, n_all + n_ctx - 1 - ci)

    def over_stacks(fn, sizes):
        group = _pick_tile(nq, sizes)

        def body(gj, carry):
            for k in range(group):
                fn(gj * group + k)
            return carry
        lax.fori_loop(0, nq // group, body, 0)

    def chunk_local(hh):
        h = pl.program_id(1) * GDN_PAIR + hh
        cols = slice(hh * LANES, (hh + 1) * LANES)
        pick = (rr == cc4 * GDN_HEADS + h).astype(BF16)

        def prep(ci, carry):
            r0 = ci * cp
            rows = pl.ds(pl.multiple_of(r0, cp), cp)
            qk = _silu(_conv_rows(qk_ref, r0, cp, seq, ctx_len, cw_qk_ref[:, cols], GDN_CONV // 2, cols))
            va = _silu(_conv_rows(v_ref, r0, cp, seq, ctx_len, cw_v_ref[:, cols], GDN_CONV // 2, cols))
            sq = qk * qk
            s_q = jnp.sum(jnp.where(is_q, sq, 0.0), axis=-1, keepdims=True)
            s_k = jnp.sum(jnp.where(is_q, 0.0, sq), axis=-1, keepdims=True)
            inv = lax.rsqrt(jnp.where(is_q, s_q, s_k) + NORM_EPS)
            qkn_ref[rows, :] = qk * inv * jnp.where(is_q, GDN_DK ** -0.5, 1.0)
            va_ref[rows, :] = va
            return carry

        lax.fori_loop(0, seq // cp, prep, 0)

        def setup(gi):
            rows2 = pl.ds(pl.multiple_of(gi * 2 * c, 2 * c), 2 * c)
            qkn = qkn_ref[rows2, :]
            va = va_ref[rows2, :]
            ab = _dot(misc_ref[0, rows2, :], pick)
            loga, beta = [], []
            for d in range(2):
                a_log = par_ref[hh, 2 * d:2 * d + 1, :]
                dt_b = par_ref[hh, 2 * d + 1:2 * d + 2, :]
                loga.append(-jnp.exp(a_log) * jax.nn.softplus(ab[:, d * LANES:(d + 1) * LANES] + dt_b))
                beta.append(jax.nn.sigmoid(ab[:, (2 + d) * LANES:(3 + d) * LANES]))
            loga = stack4(loga[0], loga[1])
            beta = stack4(beta[0], beta[1])
            qk4 = stack4(qkn, qkn)
            kq4 = pltpu.roll(qk4, GDN_DK, 1)
            q2 = jnp.where(low, qk4, kq4)
            k2 = jnp.where(low, kq4, qk4)
            v4 = stack4(va, va)

            hi, mid, _ = _split3(loga)
            r = _dot(a3_ref[...], jnp.concatenate([hi, mid], axis=0))
            cum, total = r[:cq], r[cq:]
            chi, cmid, _ = _split3(cum)
            cum_row = _dot_nt(jnp.concatenate([e_hi, e_hi], axis=1),
                              jnp.concatenate([chi, cmid], axis=1))
            dec = jnp.exp(jnp.minimum(jnp.concatenate([cum, cum], axis=1) - cum_row, 0.0))
            k_one = jnp.where(low, k2, 0.0).astype(BF16)
            kq = _dot_nt(jnp.concatenate([k2.astype(BF16), q2.astype(BF16)], axis=0), k_one)
            base = (jnp.concatenate([beta, beta], axis=1) * dec) * kq[:cq]
            base_ref[gi] = base.astype(BF16)
            xs_ref[gi] = eye - base * lv_ref[0].astype(F32)
            ecum = jnp.exp(cum)
            rhs_ref[gi] = jnp.concatenate([beta * ecum * k2, beta * v4], axis=1).astype(BF16)
            qkd = (kq[cq:] * (incl_ref[...] * dec)).astype(BF16)
            qt = jnp.where(own_half, q2 * ecum, 0.0).astype(BF16)
            kt = (k2 * jnp.exp(total - cum)).astype(BF16)
            ktt = _dot_nt(eye_k, kt)
            etot = jnp.exp(total)
            for j in range(2):
                for d in range(2):
                    lo_r = (2 * j + d) * c
                    st = step_of(2 * gi + j, d)
                    lhs1_ref[hh, st, (2 + d) * c:(3 + d) * c, :] = qt[lo_r:lo_r + c]
                    lhs2_ref[hh, st, d * c:(d + 1) * c, :] = qkd[lo_r:lo_r + c, j * LANES:(j + 1) * LANES]
                    lhs2_ref[hh, st, (2 + d) * c:(3 + d) * c, :] = jnp.where(
                        half_t[d], ktt[:, j * LANES:(j + 1) * LANES], 0.0).astype(BF16)
                    el_ref[hh, d, st] = etot[lo_r:lo_r + 8]

        over_stacks(setup, (GDN_GROUP, 1))

        for l in range(1, lv_ref.shape[0]):
            def left(gi, l=l):
                t_ref[gi] = _dot(xs_ref[gi].astype(BF16), base_ref[gi] * lv_ref[l]).astype(BF16)

            def right(gi):
                x = xs_ref[gi]
                xs_ref[gi] = x - _dot(t_ref[gi], x.astype(BF16))

            over_stacks(left, (2 * GDN_GROUP, GDN_GROUP, 1))
            over_stacks(right, (2 * GDN_GROUP, GDN_GROUP, 1))

        def finish(gi):
            wu = _dot(xs_ref[gi].astype(BF16), rhs_ref[gi])
            w = jnp.where(own_half, wu[:, :LANES], 0.0).astype(BF16)
            for j in range(2):
                for d in range(2):
                    lo_r = (2 * j + d) * c
                    st = step_of(2 * gi + j, d)
                    lhs1_ref[hh, st, d * c:(d + 1) * c, :] = w[lo_r:lo_r + c]
                    u0_ref[hh, st, d * c:(d + 1) * c, :] = wu[lo_r:lo_r + c, LANES:]

        over_stacks(finish, (GDN_GROUP, 1))

    for hh in range(GDN_PAIR):
        chunk_local(hh)

    is_fwd_row = lax.broadcasted_iota(jnp.int32, (2 * GDN_DK, LANES), 0) < GDN_DK

    def step(it, states):
        cf, cb = _chunk_order(it, n_ctx, n_all)
        rows_f = pl.ds(pl.multiple_of(cf * c, c), c)
        rows_b = pl.ds(pl.multiple_of(cb * c, c), c)
        r1 = [_dot(lhs1_ref[hh, it], states[hh].astype(BF16)) for hh in range(GDN_PAIR)]
        u = [u0_ref[hh, it] - r1[hh][:2 * c] for hh in range(GDN_PAIR)]
        r2 = [_dot(lhs2_ref[hh, it], u[hh].astype(BF16)) for hh in range(GDN_PAIR)]
        new = []
        for hh in range(GDN_PAIR):
            o2 = r1[hh][2 * c:] + r2[hh][:2 * c]
            acc_ref[hh, 0, rows_f, :] = o2[:c]
            acc_ref[hh, 1, rows_b, :] = o2[c:]
            el2 = jnp.where(is_fwd_row, el_ref[hh, 0, it][0:1], el_ref[hh, 1, it][0:1])
            new.append(el2 * states[hh] + r2[hh][2 * c:])
        return tuple(new)

    zero = jnp.zeros((2 * GDN_DK, GDN_DV), F32)
    lax.fori_loop(0, n_all, step, (zero,) * GDN_PAIR)

    def readout(ci, carry):
        rows = pl.ds(pl.multiple_of(ci * cp, cp), cp)
        for hh in range(GDN_PAIR):
            cols = slice(hh * LANES, (hh + 1) * LANES)
            o = acc_ref[hh, 0, rows, :] + acc_ref[hh, 1, rows, :]
            o = o * lax.rsqrt(jnp.mean(o * o, axis=-1, keepdims=True) + NORM_EPS)
            o_ref[0, rows, cols] = (o * nw_ref[...] * _silu(g_ref[0, rows, cols].astype(F32))).astype(o_ref.dtype)
        return carry

    lax.fori_loop(0, seq // cp, readout, 0)


def _gdn(z, conv_w, par, norm_w, ctx_len):
    b, s, _ = z.shape
    c = GLA_CHUNK
    n = s // c
    assert n % 2 == 0
    a3, lv, incl = _gdn_consts(c)
    a3 = jnp.asarray(np.tile(a3, (1, 2)), BF16)
    lv = jnp.asarray(lv, BF16)
    incl = jnp.asarray(incl, F32)
    kern = functools.partial(_gdn_pair_kernel, ctx_len=ctx_len, seq=s)
    hp = GDN_PAIR
    wide = hp * LANES

    def col(blk):
        return pl.BlockSpec((1, s, wide), lambda i, p: (i, 0, blk // hp + p))

    def full(a):
        nd = a.ndim
        return pl.BlockSpec(a.shape, lambda i, p: (0,) * nd)

    return pl.pallas_call(
        kern,
        grid=(b, GDN_HEADS // hp),
        in_specs=[col(BLK_GDN_QK), col(BLK_GDN_V), col(BLK_GDN_G),
                  pl.BlockSpec((1, s, LANES), lambda i, p: (i, 0, BLK_GDN_MISC)),
                  pl.BlockSpec((GDN_CONV, wide), lambda i, p: (0, p)),
                  pl.BlockSpec((GDN_CONV, wide), lambda i, p: (0, GDN_HEADS // hp + p)),
                  pl.BlockSpec((hp, 8, LANES), lambda i, p: (p, 0, 0)),
                  pl.BlockSpec((1, LANES), lambda i, p: (0, 0)),
                  full(a3), full(lv), full(incl)],
        out_specs=pl.BlockSpec((1, s, wide), lambda i, p: (i, 0, p)),
        out_shape=jax.ShapeDtypeStruct((b, s, GDN_HEADS * GDN_DV), BF16),
        scratch_shapes=[pltpu.VMEM((s, LANES), F32),
                        pltpu.VMEM((s, LANES), F32),
                        pltpu.VMEM((hp, 2, s, LANES), F32),
                        pltpu.VMEM((hp, n, 4 * c, 2 * GDN_DK), BF16),
                        pltpu.VMEM((hp, n, 4 * c, 2 * c), BF16),
                        pltpu.VMEM((hp, n, 2 * c, GDN_DV), F32),
                        pltpu.VMEM((hp, 2, n, 8, LANES), F32),
                        pltpu.VMEM((n // 2, 4 * c, 4 * c), BF16),
                        pltpu.VMEM((n // 2, 4 * c, 4 * c), F32),
                        pltpu.VMEM((n // 2, 4 * c, 4 * c), BF16),
                        pltpu.VMEM((n // 2, 4 * c, LANES + GDN_DV), BF16)],
        compiler_params=pltpu.CompilerParams(
            dimension_semantics=("parallel", "parallel"), vmem_limit_bytes=VMEM_LIMIT),
        name="gated_deltanet",
    )(z, z, z, z, conv_w, conv_w, par, norm_w, a3, lv, incl)


def _lru_kernel(x_ref, gate_ref, cw_ref, cb_ref, wd_ref, bd_ref, lam_ref, o_ref,
                af_ref, uf_ref, ab_ref, ub_ref, hf_ref, hb_ref, *, ctx_len, seq, c):
    w = LRU_WIDTH
    n_all, n_ctx = seq // c, ctx_len // c

    def gates(ci, d, a_ref, u_ref):
        r0 = ci * c
        xc = _conv_rows(x_ref, r0, c, seq, ctx_len, cw_ref[...], LRU_CONV // 2) + cb_ref[...]
        y = _dot(xc.astype(BF16), wd_ref[d]) + bd_ref[d]
        r = jax.nn.sigmoid(y[:, :w])
        i = jax.nn.sigmoid(y[:, w:])
        log_a = -LRU_C * r * jax.nn.softplus(-lam_ref[d])
        a = jnp.exp(log_a)
        a_ref[...] = a
        u_ref[...] = jnp.sqrt(1.0 - a * a) * (i * xc)

    def step(it, carry):
        h_f, h_b = carry
        cf, cb = _chunk_order(it, n_ctx, n_all)
        gates(cf, 0, af_ref, uf_ref)
        gates(cb, 1, ab_ref, ub_ref)
        base_f = cf * c
        base_b = cb * c

        def row(t, hh):
            hf, hb = hh
            tb = c - 1 - t
            hf = af_ref[pl.ds(t, 1), :] * hf + uf_ref[pl.ds(t, 1), :]
            hb = ab_ref[pl.ds(tb, 1), :] * hb + ub_ref[pl.ds(tb, 1), :]
            hf_ref[pl.ds(base_f + t, 1), :] = hf
            hb_ref[pl.ds(base_b + tb, 1), :] = hb
            return hf, hb

        return lax.fori_loop(0, c, row, (h_f, h_b), unroll=8)

    zero = jnp.zeros((1, w), F32)
    lax.fori_loop(0, n_all, step, (zero, zero))

    def readout(ci, carry):
        rows = pl.ds(pl.multiple_of(ci * c, c), c)
        hsum = hf_ref[rows, :] + hb_ref[rows, :]
        o_ref[0, rows, :] = (jax.nn.gelu(gate_ref[0, rows, :].astype(F32)) * hsum).astype(o_ref.dtype)
        return carry

    lax.fori_loop(0, n_all, readout, 0)


def _rglru(z, conv_w, conv_b, w_dense, b_dense, lam, ctx_len):
    b, s, _ = z.shape
    w = LRU_WIDTH
    c = _pick_tile(math.gcd(ctx_len, s), (LRU_ROWS, 128, 64))
    kern = functools.partial(_lru_kernel, ctx_len=ctx_len, seq=s, c=c)

    def full(a):
        nd = a.ndim
        return pl.BlockSpec(a.shape, lambda i: (0,) * nd)

    return pl.pallas_call(
        kern,
        grid=(b,),
        in_specs=[pl.BlockSpec((1, s, w), lambda i: (i, 0, BLK_LRU_X * LANES // w)),
                  pl.BlockSpec((1, s, w), lambda i: (i, 0, BLK_LRU_G * LANES // w)),
                  full(conv_w), full(conv_b), full(w_dense), full(b_dense), full(lam)],
        out_specs=pl.BlockSpec((1, s, w), lambda i: (i, 0, 0)),
        out_shape=jax.ShapeDtypeStruct((b, s, w), BF16),
        scratch_shapes=[pltpu.VMEM((c, w), F32)] * 4 + [pltpu.VMEM((s, w), F32)] * 2,
        compiler_params=pltpu.CompilerParams(
            dimension_semantics=("parallel",), vmem_limit_bytes=VMEM_LIMIT),
        name="rglru",
    )(z, z, conv_w, conv_b, w_dense, b_dense, lam)


def _merge_kernel(x_ref, b0_ref, b1_ref, b2_ref, b3_ref, gate_ref, gl_ref, gc_ref, wb_ref, wo_ref,
                  lng_ref, lnb_ref, o_ref, *, ctx_len, tm, alpha):
    t = pl.program_id(1)
    merged = None
    for j, br in enumerate((b0_ref, b1_ref, b2_ref, b3_ref)):
        gate = jax.nn.sigmoid(gate_ref[0, :, j * D_MODEL:(j + 1) * D_MODEL].astype(F32))
        term = gate * _dot(br[0], wb_ref[j])
        merged = term if merged is None else merged + term
    y = _dot(merged.astype(BF16), wo_ref[...])
    rows = _row_ids((tm, 1), t * tm)
    g1 = jnp.where(rows < ctx_len, gc_ref[...], gl_ref[0])
    o_ref[0] = _ln0(alpha * x_ref[0] + g1 * y) * lng_ref[...] + lnb_ref[...]


def _merge(h, branches, z, g_l, g_c, w_branch, w_out, ln_g, ln_b, ctx_len, alpha):
    b, s, d = h.shape
    tm = _pick_tile(s, (256, 128))
    kern = functools.partial(_merge_kernel, ctx_len=ctx_len, tm=tm, alpha=alpha)
    bw = BRANCH_WIDTH

    def tile(width, blk=0):
        return pl.BlockSpec((1, tm, width), lambda i, t: (i, t, blk))

    def full(a):
        nd = a.ndim
        return pl.BlockSpec(a.shape, lambda i, t: (0,) * nd)

    return pl.pallas_call(
        kern,
        grid=(b, s // tm),
        in_specs=[tile(d), tile(bw), tile(bw), tile(bw), tile(bw), tile(N_BRANCH * d, BLK_MG),
                  pl.BlockSpec((1, 1, d), lambda i, t: (i, 0, 0)), full(g_c),
                  full(w_branch), full(w_out), full(ln_g), full(ln_b)],
        out_specs=tile(d),
        out_shape=jax.ShapeDtypeStruct((b, s, d), F32),
        compiler_params=pltpu.CompilerParams(
            dimension_semantics=("parallel", "parallel"), vmem_limit_bytes=VMEM_LIMIT),
        name="merge",
    )(h, *branches, z, g_l, g_c, w_branch, w_out, ln_g, ln_b)


FFN_ROWS = 128
FFN_COLS = 256


def _ffn_kernel(x_ref, xp_ref, xn_ref, shl_ref, scl_ref, gl_ref, shc_ref, scc_ref, gc_ref,
                wu_ref, cw_ref, cb_ref, wd_ref, lng_ref, lnb_ref,
                o_ref, xm_ref, uv_ref, ug_ref, act_ref, *, ctx_len, seq, tm, alpha):
    d_ff = wd_ref.shape[0]
    fc = _pick_tile(d_ff, (FFN_COLS, LANES))
    rb = _pick_tile(tm, (FFN_ROWS, 8))
    r0 = pl.program_id(1) * tm

    x = x_ref[0]
    x_ext = jnp.concatenate([xp_ref[0], x, xn_ref[0]], axis=0)
    xm = _modulated(x_ext, _row_ids((tm + 16, 1), r0 - 8), ctx_len,
                    shl_ref[0], scl_ref[0], shc_ref[...], scc_ref[...])
    xm_ref[...] = xm.astype(BF16)

    rows = _row_ids((tm, 1), r0)
    m_up = jnp.where((rows == 0) | (rows == ctx_len), 0.0, 1.0)
    m_dn = jnp.where((rows == ctx_len - 1) | (rows == seq - 1), 0.0, 1.0)

    def conv(u_ref, a, mu, md, cw, cb):
        up = u_ref[7 + a:7 + a + rb, :]
        mid = u_ref[8 + a:8 + a + rb, :]
        dn = u_ref[9 + a:9 + a + rb, :]
        return cw[0:1] * (mu * up) + cw[1:2] * mid + cw[2:3] * (md * dn) + cb

    for j in range(d_ff // fc):
        cv = slice(j * fc, (j + 1) * fc)
        cg = slice(d_ff + j * fc, d_ff + (j + 1) * fc)
        uv_ref[...] = _dot(xm_ref[...], wu_ref[:, cv])
        ug_ref[...] = _dot(xm_ref[...], wu_ref[:, cg])
        for a in range(0, tm, rb):
            mu, md = m_up[a:a + rb], m_dn[a:a + rb]
            val = conv(uv_ref, a, mu, md, cw_ref[:, cv], cb_ref[:, cv])
            gate = conv(ug_ref, a, mu, md, cw_ref[:, cg], cb_ref[:, cg])
            act_ref[a:a + rb, cv] = (_silu(gate) * val).astype(BF16)

    f = _dot(act_ref[...], wd_ref[...])
    g2 = jnp.where(rows < ctx_len, gc_ref[...], gl_ref[0])
    o_ref[0] = _ln0(alpha * x + g2 * f) * lng_ref[...] + lnb_ref[...]


def _conv_ffn(h, mods_l, mods_c, w_up, conv_w, conv_b, w_down, ln_g, ln_b, ctx_len, alpha):
    b, s, d = h.shape
    d_ff = w_down.shape[0]
    tm = _pick_tile(s, (768, 384, 256, 128))
    fc = _pick_tile(d_ff, (FFN_COLS, LANES))
    kern = functools.partial(_ffn_kernel, ctx_len=ctx_len, seq=s, tm=tm, alpha=alpha)
    nb8 = s // 8

    def vec_l():
        return pl.BlockSpec((1, 1, d), lambda i, t: (i, 0, 0))

    def resident(a):
        nd = a.ndim
        return pl.BlockSpec(a.shape, lambda i, t: (0,) * nd, pipeline_mode=pl.Buffered(1))

    sh_l, sc_l, g_l = mods_l
    sh_c, sc_c, g_c = mods_c
    return pl.pallas_call(
        kern,
        grid=(b, s // tm),
        in_specs=[pl.BlockSpec((1, tm, d), lambda i, t: (i, t, 0)),
                  pl.BlockSpec((1, 8, d), lambda i, t: (i, jnp.maximum(t * (tm // 8) - 1, 0), 0)),
                  pl.BlockSpec((1, 8, d), lambda i, t: (i, jnp.minimum((t + 1) * (tm // 8), nb8 - 1), 0)),
                  vec_l(), vec_l(), vec_l(), resident(sh_c), resident(sc_c), resident(g_c),
                  resident(w_up), resident(conv_w), resident(conv_b), resident(w_down),
                  resident(ln_g), resident(ln_b)],
        out_specs=pl.BlockSpec((1, tm, d), lambda i, t: (i, t, 0)),
        out_shape=jax.ShapeDtypeStruct((b, s, d), F32),
        scratch_shapes=[pltpu.VMEM((tm + 16, d), BF16),
                        pltpu.VMEM((tm + 16, fc), F32), pltpu.VMEM((tm + 16, fc), F32),
                        pltpu.VMEM((tm, d_ff), BF16)],
        compiler_params=pltpu.CompilerParams(
            dimension_semantics=("parallel", "parallel"), vmem_limit_bytes=VMEM_LIMIT),
        name="conv_ffn",
    )(h, h, h, sh_l, sc_l, g_l, sh_c, sc_c, g_c, w_up, conv_w, conv_b, w_down, ln_g, ln_b)


def _rope_tables(seq, ctx_len):
    quarter = RET_DK // 4
    inv_freq = ROPE_BASE ** (-jnp.arange(quarter, dtype=F32) / quarter)
    p = jnp.arange(seq - ctx_len)
    row = (p // GRID_W).astype(F32)
    col = (p % GRID_W).astype(F32)
    ang_r = row[:, None] * inv_freq[None, :]
    ang_c = col[:, None] * inv_freq[None, :]
    cos = jnp.concatenate([jnp.cos(ang_r)] * 2 + [jnp.cos(ang_c)] * 2, axis=-1)
    sin = jnp.concatenate([-jnp.sin(ang_r), jnp.sin(ang_r), -jnp.sin(ang_c), jnp.sin(ang_c)], axis=-1)
    cos = jnp.concatenate([jnp.ones((ctx_len, RET_DK), F32), cos], axis=0)
    sin = jnp.concatenate([jnp.zeros((ctx_len, RET_DK), F32), sin], axis=0)
    ks = RET_DK ** -0.5
    return jnp.concatenate([cos, cos * ks], axis=-1), jnp.concatenate([sin, sin * ks], axis=-1)


def _block_diag(w):
    nb, k, _ = w.shape
    eye = jnp.eye(nb, dtype=w.dtype)
    return (eye[:, None, :, None] * w[:, :, None, :]).reshape(nb * k, nb * k)


def kernel(x, c, ctx, c_ctx, w_mod, b_mod, w_in, hg_lb, hg_norm, gdn_conv, gdn_a_log, gdn_dt_bias,
           gdn_norm, lru_conv, lru_conv_b, lru_w_a, lru_b_a, lru_w_i, lru_b_i, lru_lam, w_branch,
           w_out, ln_mix_g, ln_mix_b, w_up, ffn_conv, ffn_conv_b, w_down, ln_ffn_g, ln_ffn_b):
    depth = w_in.shape[0]
    batch, seq_l, d = x.shape
    ctx_len = ctx.shape[1]
    seq = ctx_len + seq_l
    alpha = (2.0 * depth) ** 0.25

    perm, n_in = _proj_perm()
    w_in_p = _take_columns(w_in.astype(BF16), perm, n_in)
    gdn_conv_p = _take_columns(gdn_conv, _gdn_conv_perm(), gdn_conv.shape[-1])
    lb = jnp.cumsum(jax.nn.softmax(hg_lb.astype(F32), axis=0), axis=0)
    lb = lb - lb[:1]
    log_gamma = jnp.log1p(-jnp.exp2(-5.0 - jnp.arange(RET_HEADS, dtype=F32)))
    lg_rows = jnp.broadcast_to(log_gamma[:, None, None], (RET_HEADS, 1, LANES))
    cos_t, sin_t = _rope_tables(seq, ctx_len)
    w_branch_b = w_branch.astype(BF16)
    w_out_b = w_out.astype(BF16)
    w_up_b = w_up.astype(BF16)
    w_down_b = w_down.astype(BF16)

    cc = jnp.concatenate([c, c_ctx[None, :], jnp.zeros((7, d), F32)], axis=0)
    mod = _modulation(cc, w_mod, b_mod)

    h = jnp.concatenate([ctx, x], axis=1)
    for i in range(depth):
        ml = mod[i, :batch].reshape(batch, N_MOD, 1, d)
        mc = mod[i, batch].reshape(N_MOD, 1, d)
        sh1, sc1, g1, sh2, sc2, g2 = (ml[:, k] for k in range(N_MOD))
        csh1, csc1, cg1, csh2, csc2, cg2 = (mc[k] for k in range(N_MOD))

        z = _in_projection(h, sh1, sc1, csh1, csc1, w_in_p[i], ctx_len)

        ret = _retention(z, lg_rows, cos_t, sin_t, ctx_len)
        hg = _hgrn2(z, lb[i, 0][None, :], lb[i, 1][None, :], hg_norm[i][None, :], ctx_len)
        par = jnp.stack([gdn_a_log[i, 0], gdn_dt_bias[i, 0], gdn_a_log[i, 1], gdn_dt_bias[i, 1]], axis=1)
        par = jnp.broadcast_to(jnp.pad(par, ((0, 0), (0, 4)))[:, :, None], (GDN_HEADS, 8, LANES))
        gd = _gdn(z, gdn_conv_p[i], par, gdn_norm[i][None, :], ctx_len)
        w_dense = jnp.stack([
            jnp.concatenate([_block_diag(lru_w_a[i, dd]), _block_diag(lru_w_i[i, dd])], axis=1)
            for dd in range(2)]).astype(BF16)
        b_dense = jnp.stack([jnp.concatenate([lru_b_a[i, dd], lru_b_i[i, dd]])[None, :] for dd in range(2)])
        lr = _rglru(z, lru_conv[i], lru_conv_b[i][None, :], w_dense, b_dense,
                    lru_lam[i].astype(F32)[:, None, :], ctx_len)

        h = _merge(h, (ret, hg, gd, lr), z, g1, cg1, w_branch_b[i], w_out_b[i],
                   ln_mix_g[i][None, :], ln_mix_b[i][None, :], ctx_len, alpha)
        h = _conv_ffn(h, (sh2, sc2, g2), (csh2, csc2, cg2), w_up_b[i], ffn_conv[i],
                      ffn_conv_b[i][None, :], w_down_b[i], ln_ffn_g[i][None, :], ln_ffn_b[i][None, :],
                      ctx_len, alpha)
    return h[:, ctx_len:]
```
